```python
import numpy as np
import jax
import jax.numpy as jnp
from jax import lax

D_MODEL = 2048
BATCH = 16
SEQ = 2048
DEPTH = 1
DEC_BATCH = 32
DEC_SEQ = 1
PAST_LEN = 16384
PAGE_SIZE = 128

D_HG = D_MODEL // 2
D_NSA = D_MODEL - D_HG
HG_DK = 128
HG_DV = 128
HG_HEADS = D_HG // HG_DK
HG_CHUNK = 32
NSA_DH = 64
NSA_HEADS = D_NSA // NSA_DH
NSA_KV_HEADS = 4
NSA_GROUP = NSA_HEADS // NSA_KV_HEADS
KV_DIM = NSA_KV_HEADS * NSA_DH
CMP_L = 32
CMP_D = 16
CMP_R = CMP_L // CMP_D
CMP_HID = 2 * NSA_DH
SLC_L = 64
N_SEL = 16
WINDOW = 512
SLC_QBLK = 16
WIN_QBLK = 128
D_IN = 4 * D_HG + 2 * D_NSA + 6 * KV_DIM + 3 * NSA_HEADS
ALPHA = (2.0 * DEPTH) ** 0.25
BETA = (8.0 * DEPTH) ** -0.25
LN_EPS = 1e-5
RMS_EPS = 1e-6
FORCED_SCORE = 1e4

kernel_name = 'hymba_hgrn2_nsa_decode_step'


def masked_softmax(s, mask, axes):
    s = jnp.where(mask, s.astype(jnp.float32), -jnp.inf)
    m = jnp.max(s, axis=axes, keepdims=True)
    m = jnp.where(jnp.isfinite(m), m, 0.0)
    p = jnp.exp(s - m)
    den = jnp.sum(p, axis=axes, keepdims=True)
    return p / jnp.where(den > 0, den, 1.0)


def split_proj(x, w_in):
    h = jnp.einsum('btd,de->bte', x, w_in)
    sizes = [D_HG] * 4 + [D_NSA, 6 * KV_DIM, 3 * NSA_HEADS, D_NSA]
    offs = np.cumsum(sizes)[:-1].tolist()
    return jnp.split(h, offs, axis=-1)


def hgrn2_chunk(S, inp):
    q, k, v, logf = inp
    C = q.shape[1]
    A = jnp.cumsum(logf, axis=1)
    causal = jnp.tril(jnp.ones((C, C), dtype=bool))[None, :, :, None, None]
    decay = jnp.exp(jnp.where(causal, A[:, :, None] - A[:, None, :], -jnp.inf))
    attn = jnp.einsum('bthk,btshk,bshk->bhts', q, decay, k)
    o = jnp.einsum('bhts,bshv->bthv', attn, v) + jnp.einsum('bthk,bhkv->bthv', q * jnp.exp(A), S)
    a_last = A[:, -1]
    S = S * jnp.exp(a_last)[..., None] + jnp.einsum('bshk,bshv->bhkv', k * jnp.exp(a_last[:, None] - A), v)
    return S, o


def hgrn2_mix(hq, hf, hi, hgate, S0, lb, norm_g):
    B, T, _ = hq.shape
    f_raw = hf.astype(jnp.float32)
    logf = jnp.log(lb + (1.0 - lb) * jax.nn.sigmoid(f_raw))
    k = (1.0 - lb) * jax.nn.sigmoid(-f_raw)
    q = hq.astype(jnp.float32)
    v = hi.astype(jnp.float32)
    C = HG_CHUNK if T % HG_CHUNK == 0 else T
    n = T // C

    def to_chunks(a, d):
        return jnp.moveaxis(a.reshape(B, n, C, HG_HEADS, d), 1, 0)

    S, o = lax.scan(hgrn2_chunk, S0.astype(jnp.float32),
                    (to_chunks(q, HG_DK), to_chunks(k, HG_DK), to_chunks(v, HG_DV), to_chunks(logf, HG_DK)))
    o = jnp.moveaxis(o, 0, 1).reshape(B, T, HG_HEADS, HG_DV)
    o = o * lax.rsqrt(jnp.mean(o * o, axis=-1, keepdims=True) + RMS_EPS)
    o = o.reshape(B, T, D_HG) * norm_g * jax.nn.silu(hgate.astype(jnp.float32))
    return o, S


def compress(rows, w1, w2, pe):
    B, T = rows.shape[:2]
    n_sub = T // CMP_D
    n_cmp = n_sub - CMP_R + 1
    sub = rows[:, :n_sub * CMP_D].reshape(B, n_sub, CMP_D, NSA_KV_HEADS, NSA_DH)
    proj = jnp.einsum('bnjhd,rjde->rbnhe', sub, w1.reshape(CMP_R, CMP_D, NSA_DH, CMP_HID))
    h = jnp.einsum('jd,jde->e', pe, w1)
    for r in range(CMP_R):
        h = h + proj[r, :, r:r + n_cmp]
    return jnp.einsum('bnhe,ed->bnhd', jax.nn.silu(h), w2)


def cmp_attention(q, kc, vc, qpos):
    n_cmp = kc.shape[1]
    end = jnp.arange(n_cmp) * CMP_D + CMP_L - 1
    s = jnp.einsum('bqhgd,bnhd->bqhgn', q, kc) * NSA_DH ** -0.5
    mask = (end[None, :] <= qpos[:, None])[None, :, None, None, :]
    p = masked_softmax(s, mask, -1)
    return jnp.einsum('bqhgn,bnhd->bqhgd', p, vc), p


def select_blocks(p, qpos, n_slc):
    n_cmp = p.shape[-1]
    ci = np.arange(n_cmp)[:, None] * CMP_D
    sj = np.arange(n_slc)[None, :] * SLC_L
    overlap = jnp.asarray(((ci < sj + SLC_L) & (ci + CMP_L > sj)).astype(np.float32))
    imp = jnp.einsum('bqhn,nj->bqhj', p.sum(axis=3), overlap)
    jb = jnp.arange(n_slc)[None, :]
    cur = (qpos // SLC_L)[:, None]
    forced = (jb == 0) | (jb == cur) | (jb == cur - 1)
    valid = jb * SLC_L <= qpos[:, None]
    score = jnp.where(forced[None, :, None, :], FORCED_SCORE, jnp.where(valid[None, :, None, :], imp, -1.0))
    _, sel = lax.top_k(score, min(N_SEL, n_slc))
    return sel


def to_blocks(rows):
    B, T = rows.shape[:2]
    n_slc = -(-T // SLC_L)
    rows = jnp.pad(rows, ((0, 0), (0, n_slc * SLC_L - T), (0, 0), (0, 0)))
    return rows.reshape(B, n_slc, SLC_L, NSA_KV_HEADS, NSA_DH).transpose(0, 3, 1, 2, 4)


def selected_attention(q, kb, vb, sel, qpos):
    B = q.shape[0]
    bi = jnp.arange(B)[:, None, None, None]
    hi = jnp.arange(NSA_KV_HEADS)[None, None, :, None]
    kg = kb[bi, hi, sel]
    vg = vb[bi, hi, sel]
    s = jnp.einsum('bqhgd,bqhkld->bqhgkl', q, kg) * NSA_DH ** -0.5
    kpos = sel[..., None] * SLC_L + jnp.arange(SLC_L)
    mask = (kpos <= qpos[None, :, None, None, None])[:, :, :, None]
    p = masked_softmax(s, mask, (-2, -1))
    return jnp.einsum('bqhgkl,bqhkld->bqhgd', p, vg)


def window_attention(q, k, v, qpos, kpos):
    s = jnp.einsum('bqhgd,bkhd->bqhgk', q, k) * NSA_DH ** -0.5
    rel = qpos[:, None] - kpos[None, :]
    mask = ((rel >= 0) & (rel < WINDOW) & (kpos[None, :] >= 0))[None, :, None, None, :]
    p = masked_softmax(s, mask, -1)
    return jnp.einsum('bqhgk,bkhd->bqhgd', p, v)


def blocked_over_queries(fn, T, blk):
    out = lax.map(fn, jnp.arange(T // blk))
    return jnp.moveaxis(out, 0, 1).reshape((out.shape[1], T) + out.shape[3:])


def nsa_cmp_slc(q, rows, qpos, cmp_w1, cmp_w2, cmp_pe, blocked):
    kc = compress(rows[:, :, 0], cmp_w1[0], cmp_w2[0], cmp_pe[0])
    vc = compress(rows[:, :, 1], cmp_w1[1], cmp_w2[1], cmp_pe[1])
    o_cmp, p = cmp_attention(q, kc, vc, qpos)
    kb = to_blocks(rows[:, :, 2])
    vb = to_blocks(rows[:, :, 3])
    sel = select_blocks(p, qpos, kb.shape[2])
    if blocked:
        def one(j):
            st = j * SLC_QBLK
            return selected_attention(lax.dynamic_slice_in_dim(q, st, SLC_QBLK, 1), kb, vb,
                                      lax.dynamic_slice_in_dim(sel, st, SLC_QBLK, 1),
                                      lax.dynamic_slice_in_dim(qpos, st, SLC_QBLK, 0))
        o_slc = blocked_over_queries(one, q.shape[1], SLC_QBLK)
    else:
        o_slc = selected_attention(q, kb, vb, sel, qpos)
    return o_cmp, o_slc


def nsa_combine(o_cmp, o_slc, o_win, ngates, ngate):
    B, T = ngates.shape[:2]
    g = jax.nn.sigmoid(ngates.astype(jnp.float32)).reshape(B, T, 3, NSA_KV_HEADS, NSA_GROUP)[..., None]
    o = g[:, :, 0] * o_cmp + g[:, :, 1] * o_slc + g[:, :, 2] * o_win
    return o.reshape(B, T, D_NSA) * jax.nn.silu(ngate.astype(jnp.float32))


def post_norm(x, o_hg, o_nsa, w_out, ln_g, ln_b):
    y = jnp.einsum('bte,ed->btd', jnp.concatenate([o_hg, o_nsa], axis=-1), w_out)
    z = ALPHA * x.astype(jnp.float32) + y
    mu = jnp.mean(z, axis=-1, keepdims=True)
    var = jnp.mean(jnp.square(z - mu), axis=-1, keepdims=True)
    return ((z - mu) * lax.rsqrt(var + LN_EPS) * ln_g + ln_b).astype(x.dtype)


def prompt_layer(x, w_in, lb, hg_norm_g, cmp_w1, cmp_w2, cmp_pe, w_out, ln_g, ln_b):
    B, T, _ = x.shape
    hq, hf, hi, hgate, nq, kv, ngates, ngate = split_proj(x, w_in)
    S0 = jnp.zeros((B, HG_HEADS, HG_DK, HG_DV), jnp.float32)
    o_hg, S = hgrn2_mix(hq, hf, hi, hgate, S0, lb, hg_norm_g)
    q = nq.reshape(B, T, NSA_KV_HEADS, NSA_GROUP, NSA_DH)
    kv = kv.reshape(B, T, 6, NSA_KV_HEADS, NSA_DH)
    rows, win = kv[:, :, :4], kv[:, :, 4:]
    qpos = jnp.arange(T)
    o_cmp, o_slc = nsa_cmp_slc(q, rows, qpos, cmp_w1, cmp_w2, cmp_pe, True)
    win_pad = jnp.pad(win, ((0, 0), (WINDOW, 0), (0, 0), (0, 0), (0, 0)))

    def one(j):
        st = j * WIN_QBLK
        kw = lax.dynamic_slice_in_dim(win_pad, st, WINDOW + WIN_QBLK, 1)
        return window_attention(lax.dynamic_slice_in_dim(q, st, WIN_QBLK, 1), kw[:, :, 0], kw[:, :, 1],
                                st + jnp.arange(WIN_QBLK), st - WINDOW + jnp.arange(WINDOW + WIN_QBLK))
    o_win = blocked_over_queries(one, T, WIN_QBLK)
    o_nsa = nsa_combine(o_cmp, o_slc, o_win, ngates, ngate)
    y = post_norm(x, o_hg, o_nsa, w_out, ln_g, ln_b)
    return y, rows, win[:, T - min(WINDOW, T):], S.astype(x.dtype)


def sample_layer(x, cache_kv, cache_win, state, page_table, w_in, lb, hg_norm_g,
                 cmp_w1, cmp_w2, cmp_pe, w_out, ln_g, ln_b):
    B, T, _ = x.shape
    hq, hf, hi, hgate, nq, kv, ngates, ngate = split_proj(x, w_in)
    o_hg, S = hgrn2_mix(hq, hf, hi, hgate, state, lb, hg_norm_g)
    q = nq.reshape(B, T, NSA_KV_HEADS, NSA_GROUP, NSA_DH)
    kv = kv.reshape(B, T, 6, NSA_KV_HEADS, NSA_DH).astype(cache_kv.dtype)
    rows_new, win_new = kv[:, :, :4], kv[:, :, 4:]
    past_len = page_table.shape[1] * PAGE_SIZE
    past = cache_kv[page_table].reshape(B, past_len, 4, NSA_KV_HEADS, NSA_DH)
    rows_full = jnp.concatenate([past, rows_new], axis=1)
    qpos = past_len + jnp.arange(T)
    o_cmp, o_slc = nsa_cmp_slc(q, rows_full, qpos, cmp_w1, cmp_w2, cmp_pe, False)
    n_buf = cache_win.shape[1]
    win_cat = jnp.concatenate([cache_win, win_new.astype(cache_win.dtype)], axis=1)
    kpos = past_len - n_buf + jnp.arange(n_buf + T)
    o_win = window_attention(q, win_cat[:, :, 0], win_cat[:, :, 1], qpos, kpos)
    o_nsa = nsa_combine(o_cmp, o_slc, o_win, ngates, ngate)
    y = post_norm(x, o_hg, o_nsa, w_out, ln_g, ln_b)
    return y, rows_new, win_cat[:, T:], S.astype(state.dtype)


def setup_inputs(seed: int = 0) -> dict:
    key = jax.random.key(seed)
    ks = jax.random.split(key, 16)
    nrm = jax.random.normal
    f32 = jnp.float32
    n_pages = PAST_LEN // PAGE_SIZE
    n_pool = (DEC_BATCH * n_pages * 5) // 4
    win_buf = min(WINDOW, PAST_LEN)
    x_prompt = nrm(ks[0], (BATCH, SEQ, D_MODEL), f32)
    x_sample = nrm(ks[1], (DEC_BATCH, DEC_SEQ, D_MODEL), f32)
    cache_kv = nrm(ks[2], (DEPTH, n_pool, PAGE_SIZE, 4, NSA_KV_HEADS, NSA_DH), f32)
    cache_win = nrm(ks[3], (DEPTH, DEC_BATCH, win_buf, 2, NSA_KV_HEADS, NSA_DH), f32)
    state_hgrn = 0.5 * nrm(ks[4], (DEPTH, DEC_BATCH, HG_HEADS, HG_DK, HG_DV), f32)
    page_table = jax.random.permutation(ks[5], n_pool)[:DEC_BATCH * n_pages].reshape(DEC_BATCH, n_pages).astype(jnp.int32)
    w_in = nrm(ks[6], (DEPTH, D_MODEL, D_IN), f32) * D_MODEL ** -0.5
    hg_lb = nrm(ks[7], (DEPTH + 1, D_HG), f32)
    hg_norm_g = 1.0 + 0.02 * nrm(ks[8], (DEPTH, D_HG), f32)
    cmp_w1 = nrm(ks[9], (DEPTH, 2, CMP_L, NSA_DH, CMP_HID), f32) * (CMP_L * NSA_DH) ** -0.5
    cmp_w2 = nrm(ks[10], (DEPTH, 2, CMP_HID, NSA_DH), f32) * CMP_HID ** -0.5
    cmp_pe = 0.5 * nrm(ks[11], (DEPTH, 2, CMP_L, NSA_DH), f32)
    w_out = nrm(ks[12], (DEPTH, D_MODEL, D_MODEL), f32) * (BETA * D_MODEL ** -0.5)
    ln_g = 1.0 + 0.02 * nrm(ks[13], (DEPTH, D_MODEL), f32)
    ln_b = 0.02 * nrm(ks[14], (DEPTH, D_MODEL), f32)
    return {'x_prompt': x_prompt, 'x_sample': x_sample, 'cache_kv': cache_kv, 'cache_win': cache_win,
            'state_hgrn': state_hgrn, 'page_table': page_table, 'w_in': w_in, 'hg_lb': hg_lb,
            'hg_norm_g': hg_norm_g, 'cmp_w1': cmp_w1, 'cmp_w2': cmp_w2, 'cmp_pe': cmp_pe,
            'w_out': w_out, 'ln_g': ln_g, 'ln_b': ln_b}


def reference(x_prompt, x_sample, cache_kv, cache_win, state_hgrn, page_table, w_in, hg_lb,
              hg_norm_g, cmp_w1, cmp_w2, cmp_pe, w_out, ln_g, ln_b):
    lbs = jnp.cumsum(jax.nn.softmax(hg_lb.astype(jnp.float32), axis=0), axis=0)
    yp, ys = x_prompt, x_sample
    kvp, winp, hgp, kvs, wins, hgs = [], [], [], [], [], []
    for layer in range(DEPTH):
        yp, a, b, c = prompt_layer(yp, w_in[layer], lbs[layer], hg_norm_g[layer], cmp_w1[layer],
                                   cmp_w2[layer], cmp_pe[layer], w_out[layer], ln_g[layer], ln_b[layer])
        kvp.append(a)
        winp.append(b)
        hgp.append(c)
        ys, a, b, c = sample_layer(ys, cache_kv[layer], cache_win[layer], state_hgrn[layer], page_table,
                                   w_in[layer], lbs[layer], hg_norm_g[layer], cmp_w1[layer], cmp_w2[layer],
                                   cmp_pe[layer], w_out[layer], ln_g[layer], ln_b[layer])
        kvs.append(a)
        wins.append(b)
        hgs.append(c)
    kv_prompt = jnp.stack(kvp)
    win_prompt = jnp.stack(winp)
    hgrn_prompt = jnp.stack(hgp)
    kv_sample = jnp.stack(kvs)
    win_sample = jnp.stack(wins)
    hgrn_sample = jnp.stack(hgs)
    return (yp, ys, kv_prompt, win_prompt, hgrn_prompt, kv_sample, win_sample, hgrn_sample)
```

```python
import functools

import numpy as np
import jax
import jax.numpy as jnp
from jax import lax
from jax.experimental import pallas as pl
from jax.experimental.pallas import tpu as pltpu

F32 = jnp.float32
BF16 = jnp.bfloat16

D_MODEL = 2048
D_HG = 1024
D_NSA = 1024
HG_DK = 128
HG_DV = 128
HG_HEADS = 8
NSA_DH = 64
NSA_HEADS = 16
NSA_KV_HEADS = 4
NSA_GROUP = 4
KV_DIM = 256
CMP_L = 32
CMP_D = 16
CMP_R = 2
CMP_HID = 128
SLC_L = 64
N_SEL = 16
WINDOW = 512
PAGE_SIZE = 128
ALPHA = 2.0 ** 0.25
LN_EPS = 1e-5
RMS_EPS = 1e-6
FORCED_SCORE = 1e4
NEG = -1e30

LANES = 128
SUBLANES = 8
VMEM_LIMIT = 56 * 1024 * 1024
SROWS = 16

COL_TILE = 512
N_HG_TILES = 4 * D_HG // COL_TILE
N_Q_TILES = D_NSA // COL_TILE
N_KV_TILES = 6 * KV_DIM // COL_TILE
G_WIDTH = 1536
N_G_TILES = G_WIDTH // COL_TILE
N_COL_TILES = N_HG_TILES + N_Q_TILES + N_KV_TILES + N_G_TILES
D_IN_PAD = N_COL_TILES * COL_TILE

HG_CHUNK = 128
HG_LEVELS = 4


def _cparams(sem):
    return pltpu.CompilerParams(dimension_semantics=sem, vmem_limit_bytes=VMEM_LIMIT)


def _dot(a, b):
    return jnp.dot(a, b, preferred_element_type=F32)


def _dot_nt(a, b):
    return lax.dot_general(a, b, (((1,), (1,)), ((), ())), preferred_element_type=F32)


def _split3(x):
    hi = x.astype(BF16)
    r1 = x - hi.astype(F32)
    mid = r1.astype(BF16)
    lo = (r1 - mid.astype(F32)).astype(BF16)
    return hi, mid, lo


def _dot3(a_bf, x):
    hi, mid, lo = _split3(x)
    return _dot(a_bf, hi) + _dot(a_bf, mid) + _dot(a_bf, lo)


def _dot3_nt(a_bf, x):
    hi, mid, lo = _split3(x)
    return _dot_nt(a_bf, hi) + _dot_nt(a_bf, mid) + _dot_nt(a_bf, lo)


def _silu(x):
    return x * jax.nn.sigmoid(x)


def _inproj_kernel(x_ref, w_ref, hg_ref, q_ref, kv_ref, kvb_ref, g_ref, xb_ref):
    j = pl.program_id(1)

    @pl.when(j == 0)
    def _():
        xb_ref[...] = x_ref[...].astype(BF16)

    acc = _dot(xb_ref[...], w_ref[...])
    q0 = N_HG_TILES
    kv0 = q0 + N_Q_TILES
    g0 = kv0 + N_KV_TILES

    @pl.when(j < q0)
    def _():
        hg_ref[...] = acc

    @pl.when((j >= q0) & (j < kv0))
    def _():
        q_ref[...] = acc.astype(BF16)

    @pl.when((j >= kv0) & (j < g0))
    def _():
        kv_ref[...] = acc
        kvb_ref[...] = acc.astype(BF16)

    @pl.when(j >= g0)
    def _():
        g_ref[...] = acc


def _inproj(x2d, w_p, tm):
    m = x2d.shape[0]
    assert m % tm == 0
    q0 = N_HG_TILES
    kv0 = q0 + N_Q_TILES
    g0 = kv0 + N_KV_TILES

    def clampmap(lo, n):
        return lambda i, j: (i, jnp.clip(j - lo, 0, n - 1))

    return pl.pallas_call(
        _inproj_kernel,
        grid=(m // tm, N_COL_TILES),
        in_specs=[pl.BlockSpec((tm, D_MODEL), lambda i, j: (i, 0)),
                  pl.BlockSpec((D_MODEL, COL_TILE), lambda i, j: (0, j))],
        out_specs=[pl.BlockSpec((tm, COL_TILE), clampmap(0, N_HG_TILES)),
                   pl.BlockSpec((tm, COL_TILE), clampmap(q0, N_Q_TILES)),
                   pl.BlockSpec((tm, COL_TILE), clampmap(kv0, N_KV_TILES)),
                   pl.BlockSpec((tm, COL_TILE), clampmap(kv0, N_KV_TILES)),
                   pl.BlockSpec((tm, COL_TILE), clampmap(g0, N_G_TILES))],
        out_shape=[jax.ShapeDtypeStruct((m, 4 * D_HG), F32),
                   jax.ShapeDtypeStruct((m, D_NSA), BF16),
                   jax.ShapeDtypeStruct((m, 6 * KV_DIM), F32),
                   jax.ShapeDtypeStruct((m, 6 * KV_DIM), BF16),
                   jax.ShapeDtypeStruct((m, G_WIDTH), F32)],
        scratch_shapes=[pltpu.VMEM((tm, D_MODEL), BF16)],
        compiler_params=_cparams(("arbitrary", "arbitrary")),
        name="inproj",
    )(x2d, w_p)


def _prep_w_in(w_in):
    o_nq = 4 * D_HG
    o_kv = o_nq + D_NSA
    o_gates = o_kv + 6 * KV_DIM
    o_gate = o_gates + 3 * NSA_HEADS
    pad = G_WIDTH - D_NSA - 3 * NSA_HEADS
    w = jnp.concatenate([w_in[:, :o_gates], w_in[:, o_gate:], w_in[:, o_gates:o_gate],
                         jnp.zeros((D_MODEL, pad), w_in.dtype)], axis=1)
    return w.astype(BF16)


def _outproj_kernel(x_ref, ohg_ref, onsa_ref, w_ref, g_ref, b_ref, y_ref):
    y = _dot(ohg_ref[...], w_ref[0:D_HG, :]) + _dot(onsa_ref[...], w_ref[D_HG:D_MODEL, :])
    z = ALPHA * x_ref[...] + y
    mu = jnp.mean(z, axis=-1, keepdims=True)
    zc = z - mu
    var = jnp.mean(zc * zc, axis=-1, keepdims=True)
    y_ref[...] = zc * lax.rsqrt(var + LN_EPS) * g_ref[...] + b_ref[...]


def _outproj(x2d, o_hg, o_nsa, w_out_b, ln_g, ln_b, tm):
    m = x2d.shape[0]
    assert m % tm == 0
    return pl.pallas_call(
        _outproj_kernel,
        grid=(m // tm,),
        in_specs=[pl.BlockSpec((tm, D_MODEL), lambda i: (i, 0)),
                  pl.BlockSpec((tm, D_HG), lambda i: (i, 0)),
                  pl.BlockSpec((tm, D_NSA), lambda i: (i, 0)),
                  pl.BlockSpec((D_MODEL, D_MODEL), lambda i: (0, 0)),
                  pl.BlockSpec((1, D_MODEL), lambda i: (0, 0)),
                  pl.BlockSpec((1, D_MODEL), lambda i: (0, 0))],
        out_specs=pl.BlockSpec((tm, D_MODEL), lambda i: (i, 0)),
        out_shape=jax.ShapeDtypeStruct((m, D_MODEL), F32),
        compiler_params=_cparams(("arbitrary",)),
        name="outproj_ln",
    )(x2d, o_hg, o_nsa, w_out_b, ln_g.reshape(1, D_MODEL), ln_b.reshape(1, D_MODEL))


def _hgrn_gmat():
    c = HG_CHUNK
    tri = np.tril(np.ones((c, c), np.float32))
    mats = [tri]
    for lvl in range(1, HG_LEVELS + 1):
        half = SUBLANES * 2 ** (lvl - 1)
        t = np.arange(c)
        mid = (t // (2 * half)) * (2 * half) + half
        mats.append(tri - tri[mid - 1])
    return jnp.asarray(np.concatenate(mats, axis=0), BF16)


def _hgrn_lower_bound(lb_ref):
    raw = lb_ref[...]
    mx = jnp.max(raw, axis=0, keepdims=True)
    e = jnp.exp(raw - mx)
    return e[0:1, :] / jnp.sum(e, axis=0, keepdims=True)


def _hgrn_kernel(lb_ref, ng_ref, gm_ref, q_ref, f_ref, i_ref, g_ref, o_ref, s_ref, st_ref):
    c = HG_CHUNK
    t_len = q_ref.shape[0]
    lb = _hgrn_lower_bound(lb_ref)
    ng = ng_ref[...]
    st_ref[...] = jnp.zeros_like(st_ref)

    row = lax.broadcasted_iota(jnp.int32, (c, HG_DK), 0)
    sub3 = lax.broadcasted_iota(jnp.int32, (c // SUBLANES, SUBLANES, HG_DK), 1)
    r2 = lax.broadcasted_iota(jnp.int32, (c, c), 0)
    c2 = lax.broadcasted_iota(jnp.int32, (c, c), 1)

    def chunk(ci, carry):
        sl = pl.ds(pl.multiple_of(ci * c, c), c)
        fr = f_ref[sl, :]
        q = q_ref[sl, :]
        v = i_ref[sl, :]
        logf = jnp.log(lb + (1.0 - lb) * jax.nn.sigmoid(fr))
        k = (1.0 - lb) * jax.nn.sigmoid(-fr)
        dall = _dot3(gm_ref[...], logf)
        a = dall[0:c]

        q3 = q.reshape(c // SUBLANES, SUBLANES, HG_DK)
        k3 = k.reshape(c // SUBLANES, SUBLANES, HG_DK)
        v3 = v.reshape(c // SUBLANES, SUBLANES, HG_DV)
        a3 = a.reshape(c // SUBLANES, SUBLANES, HG_DK)
        od = jnp.sum(q3 * k3, axis=-1, keepdims=True) * v3
        for d in range(1, SUBLANES):
            ks = pltpu.roll(k3, d, 1)
            as_ = pltpu.roll(a3, d, 1)
            vs = pltpu.roll(v3, d, 1)
            ok = sub3 >= d
            w = jnp.where(ok, q3 * ks * jnp.exp(jnp.minimum(a3 - as_, 0.0)), 0.0)
            od = od + jnp.sum(w, axis=-1, keepdims=True) * vs
        od = od.reshape(c, HG_DV)

        att = jnp.zeros((c, c), F32)
        for lvl in range(1, HG_LEVELS + 1):
            half = SUBLANES * 2 ** (lvl - 1)
            dl = dall[lvl * c:(lvl + 1) * c]
            e = jnp.exp(-jnp.abs(dl))
            upper = (row % (2 * half)) >= half
            qt = jnp.where(upper, q * e, 0.0).astype(BF16)
            kt = jnp.where(upper, 0.0, k * e).astype(BF16)
            al = _dot_nt(qt, kt)
            if 2 * half < c:
                al = jnp.where((r2 // (2 * half)) == (c2 // (2 * half)), al, 0.0)
            att = att + al
        vb = v.astype(BF16)
        o = od + _dot(att.astype(BF16), vb)

        st = st_ref[...]
        o = o + _dot_nt((q * jnp.exp(a)).astype(BF16), st.astype(BF16))
        alast = a[c - 1:c, :]
        kd = (k * jnp.exp(alast - a)).astype(BF16)
        st_ref[...] = st * jnp.exp(alast) + _dot(v.T.astype(BF16), kd)

        o = o * lax.rsqrt(jnp.mean(o * o, axis=-1, keepdims=True) + RMS_EPS)
        o_ref[sl, :] = (o * ng * _silu(g_ref[sl, :])).astype(o_ref.dtype)
        return carry

    lax.fori_loop(0, t_len // c, chunk, 0)
    s_ref[...] = st_ref[...].T


def _hgrn_prompt(hg, hg_lb, norm_g, b, t):
    assert t % HG_CHUNK == 0
    hg3 = hg.reshape(b, t, 4 * D_HG)
    gm = _hgrn_gmat()

    def col(k):
        return pl.BlockSpec((None, t, HG_DK), lambda bi, h: (bi, 0, k * HG_HEADS + h))

    return pl.pallas_call(
        _hgrn_kernel,
        grid=(b, HG_HEADS),
        in_specs=[pl.BlockSpec((2, HG_DK), lambda bi, h: (0, h)),
                  pl.BlockSpec((1, HG_DV), lambda bi, h: (0, h)),
                  pl.BlockSpec(gm.shape, lambda bi, h: (0, 0)),
                  col(0), col(1), col(2), col(3)],
        out_specs=[pl.BlockSpec((None, t, HG_DV), lambda bi, h: (bi, 0, h)),
                   pl.BlockSpec((None, None, HG_DK, HG_DV), lambda bi, h: (bi, h, 0, 0))],
        out_shape=[jax.ShapeDtypeStruct((b, t, D_HG), BF16),
                   jax.ShapeDtypeStruct((b, HG_HEADS, HG_DK, HG_DV), F32)],
        scratch_shapes=[pltpu.VMEM((HG_DV, HG_DK), F32)],
        compiler_params=_cparams(("arbitrary", "arbitrary")),
        name="hgrn_prompt",
    )(hg_lb, norm_g.reshape(1, D_HG), gm, hg3, hg3, hg3, hg3)


def _hgrn_step_kernel(lb_ref, ng_ref, x_ref, s_ref, o_ref, so_ref):
    raw = lb_ref[...]
    mx = jnp.max(raw, axis=0)
    e0 = jnp.exp(raw[0] - mx)
    lb_all = e0 / (e0 + jnp.exp(raw[1] - mx))
    x = x_ref[...]
    eye = (lax.broadcasted_iota(jnp.int32, (HG_DK, HG_DK), 0)
           == lax.broadcasted_iota(jnp.int32, (HG_DK, HG_DK), 1))

    def column(rowvec):
        return jnp.sum(jnp.where(eye, rowvec, 0.0), axis=1, keepdims=True)

    for h in range(HG_HEADS):
        lb = lb_all[h:h + 1]
        q = x[h:h + 1]
        fr = x[HG_HEADS + h:HG_HEADS + h + 1]
        v = x[2 * HG_HEADS + h:2 * HG_HEADS + h + 1]
        gate = x[3 * HG_HEADS + h:3 * HG_HEADS + h + 1]
        f = lb + (1.0 - lb) * jax.nn.sigmoid(fr)
        k = (1.0 - lb) * jax.nn.sigmoid(-fr)
        s_new = s_ref[h] * column(f) + column(k) * v
        so_ref[h] = s_new
        o = jnp.sum(column(q) * s_new, axis=0, keepdims=True)
        o = o * lax.rsqrt(jnp.mean(o * o, axis=-1, keepdims=True) + RMS_EPS)
        o_ref[h:h + 1, :] = o * ng_ref[h:h + 1, :] * _silu(gate)


def _hgrn_sample(hg_s, state, hg_lb, norm_g):
    b = hg_s.shape[0]
    x3 = hg_s.reshape(b, 4 * HG_HEADS, HG_DK)
    return pl.pallas_call(
        _hgrn_step_kernel,
        grid=(b,),
        in_specs=[pl.BlockSpec((2, HG_HEADS, HG_DK), lambda i: (0, 0, 0)),
                  pl.BlockSpec((HG_HEADS, HG_DV), lambda i: (0, 0)),
                  pl.BlockSpec((None, 4 * HG_HEADS, HG_DK), lambda i: (i, 0, 0)),
                  pl.BlockSpec((None, HG_HEADS, HG_DK, HG_DV), lambda i: (i, 0, 0, 0))],
        out_specs=[pl.BlockSpec((None, HG_HEADS, HG_DV), lambda i: (i, 0, 0)),
                   pl.BlockSpec((None, HG_HEADS, HG_DK, HG_DV), lambda i: (i, 0, 0, 0))],
        out_shape=[jax.ShapeDtypeStruct((b, HG_HEADS, HG_DV), F32),
                   jax.ShapeDtypeStruct((b, HG_HEADS, HG_DK, HG_DV), F32)],
        compiler_params=_cparams(("arbitrary",)),
        name="hgrn_sample",
    )(hg_lb.reshape(2, HG_HEADS, HG_DK), norm_g.reshape(HG_HEADS, HG_DV), x3, state)


def _cmp_weights(cmp_w1, cmp_w2):
    w1 = cmp_w1.reshape(2, CMP_R, CMP_D, NSA_DH, CMP_HID)
    w1 = jnp.transpose(w1, (0, 2, 3, 1, 4)).reshape(2, CMP_D, NSA_DH, CMP_R * CMP_HID)
    z1 = jnp.zeros_like(w1)
    w1p = jnp.stack([jnp.concatenate([w1, z1], axis=2), jnp.concatenate([z1, w1], axis=2)], axis=1)
    z2 = jnp.zeros_like(cmp_w2)
    w2p = jnp.stack([jnp.concatenate([cmp_w2, z2], axis=2), jnp.concatenate([z2, cmp_w2], axis=2)], axis=1)
    return w1p.astype(BF16), w2p.astype(BF16)


def _pe_term(pe_ref, w1f_ref):
    pe = jnp.broadcast_to(pe_ref[...], (SUBLANES, CMP_L * NSA_DH))
    w = w1f_ref[...]
    wh, wm, wl = _split3(w)
    ph, pm, pl_ = _split3(pe)
    out = (_dot(ph, wh) + _dot(ph, wm) + _dot(pm, wh)
           + _dot(ph, wl) + _dot(pm, wm) + _dot(pl_, wh))
    return out[0:1, :]


def _cmp_prompt_kernel(x_ref, w1p_ref, w2p_ref, pe_ref, w1f_ref, o_ref):
    n_sub = x_ref.shape[0] // CMP_D
    pe_t = _pe_term(pe_ref, w1f_ref)
    out = jnp.zeros((n_sub, LANES), F32)
    for par in range(2):
        acc = jnp.zeros((n_sub, CMP_R * CMP_HID), F32)
        for j in range(CMP_D):
            xj = x_ref[pl.ds(j, n_sub, stride=CMP_D), :].astype(BF16)
            acc = acc + _dot(xj, w1p_ref[par, j])
        nxt = pltpu.roll(acc[:, CMP_HID:], n_sub - 1, 0)
        hid = pe_t + acc[:, :CMP_HID] + nxt
        out = out + _dot(_silu(hid).astype(BF16), w2p_ref[par])
    o_ref[...] = out.astype(o_ref.dtype)


def _cmp_prompt(kv, w1p, w2p, cmp_w1, cmp_pe, b, t):
    assert t % CMP_D == 0 and (t // CMP_D) % SUBLANES == 0
    n_sub = t // CMP_D
    kv3 = kv.reshape(b, t, 6 * KV_DIM)
    pe = cmp_pe.reshape(2, 1, CMP_L * NSA_DH)
    w1f = cmp_w1.reshape(2, CMP_L * NSA_DH, CMP_HID)
    return pl.pallas_call(
        _cmp_prompt_kernel,
        grid=(b, 2, 2),
        in_specs=[pl.BlockSpec((None, t, LANES), lambda bi, w, p: (bi, 0, 2 * w + p)),
                  pl.BlockSpec((None, 2, CMP_D, LANES, CMP_R * CMP_HID), lambda bi, w, p: (w, 0, 0, 0, 0)),
                  pl.BlockSpec((None, 2, CMP_HID, LANES), lambda bi, w, p: (w, 0, 0, 0)),
                  pl.BlockSpec((None, 1, CMP_L * NSA_DH), lambda bi, w, p: (w, 0, 0)),
                  pl.BlockSpec((None, CMP_L * NSA_DH, CMP_HID), lambda bi, w, p: (w, 0, 0))],
        out_specs=pl.BlockSpec((None, None, n_sub, LANES), lambda bi, w, p: (bi, w, 0, p)),
        out_shape=jax.ShapeDtypeStruct((b, 2, n_sub, KV_DIM), BF16),
        compiler_params=_cparams(("arbitrary", "arbitrary", "arbitrary")),
        name="cmp_prompt",
    )(kv3, w1p, w2p, pe, w1f)


TQ = 128
SLC_TK = 512


def _overlap_t(n_slc, n_cmp_pad):
    ci = np.arange(n_cmp_pad)[None, :] * CMP_D
    sj = np.arange(n_slc)[:, None] * SLC_L
    return ((ci < sj + SLC_L) & (ci + CMP_L > sj)).astype(np.float32)


def _gate_expand(pair):
    m = np.zeros((3, LANES, 2 * NSA_GROUP * NSA_DH), np.float32)
    for br in range(3):
        for hp in range(2):
            for g in range(NSA_GROUP):
                src = br * NSA_HEADS + (2 * pair + hp) * NSA_GROUP + g
                c0 = (hp * NSA_GROUP + g) * NSA_DH
                m[br, src, c0:c0 + NSA_DH] = 1.0
    return m


def _place_pair(x_even, x_odd, hp, lane):
    if hp == 0:
        return jnp.where(lane < NSA_DH, x_even, pltpu.roll(x_odd, NSA_DH, 1))
    return jnp.where(lane < NSA_DH, pltpu.roll(x_even, NSA_DH, 1), x_odd)


def _nsa_prompt_kernel(q_ref, sk_ref, sv_ref, wk_ref, wv_ref, kc_ref, vc_ref, gate_ref, gates_ref,
                       ovt_ref, gex_ref, o_ref, ka_ref):
    i = pl.program_id(2)
    tq = q_ref.shape[0]
    t_len = sk_ref.shape[0]
    n_slc = ovt_ref.shape[0]
    n_cmp_pad = kc_ref.shape[0]
    rows = NSA_GROUP * tq
    n_sel = min(N_SEL, n_slc)

    @pl.when(i == 0)
    def _():
        ka_ref[:, 0:LANES] = sk_ref[...]
        kb = lax.broadcasted_iota(jnp.int32, (t_len, LANES), 0) // SLC_L
        jl = lax.broadcasted_iota(jnp.int32, (t_len, LANES), 1)
        ka_ref[:, LANES:2 * LANES] = jnp.where(kb == jl, 1.0, 0.0).astype(BF16)

    lane = lax.broadcasted_iota(jnp.int32, (tq, LANES), 1)
    qpos_r = i * tq + lax.broadcasted_iota(jnp.int32, (rows, 1), 0) % tq
    sig_gates = jax.nn.sigmoid(gates_ref[...])
    o_parts = [[], [], []]

    for hp in range(2):
        qs = []
        for g in range(NSA_GROUP):
            blk = hp * 2 + g // 2
            x = q_ref[:, blk * LANES:(blk + 1) * LANES]
            if g % 2 != hp:
                x = pltpu.roll(x, NSA_DH, 1)
            keep = (lane >= NSA_DH) if hp == 1 else (lane < NSA_DH)
            qs.append(jnp.where(keep, x, jnp.zeros_like(x)))
        qs = jnp.concatenate(qs, axis=0) * jnp.asarray(NSA_DH ** -0.5, BF16)

        s = _dot_nt(qs, kc_ref[...])
        end = lax.broadcasted_iota(jnp.int32, (rows, n_cmp_pad), 1) * CMP_D + (CMP_L - 1)
        msk = end <= qpos_r
        s = jnp.where(msk, s, NEG)
        m = jnp.max(s, axis=-1, keepdims=True)
        p = jnp.where(msk, jnp.exp(s - m), 0.0)
        den = jnp.sum(p, axis=-1, keepdims=True)
        p = p / jnp.where(den > 0, den, 1.0)
        o_c = _dot(p.astype(BF16), vc_ref[...])

        psum = p[0:tq] + p[tq:2 * tq] + p[2 * tq:3 * tq] + p[3 * tq:4 * tq]
        imp_t = _dot3_nt(ovt_ref[...], psum)
        jb = lax.broadcasted_iota(jnp.int32, (n_slc, tq), 0)
        qpos_l = i * tq + lax.broadcasted_iota(jnp.int32, (n_slc, tq), 1)
        cur = qpos_l // SLC_L
        forced = (jb == 0) | (jb == cur) | (jb == cur - 1)
        valid = jb * SLC_L <= qpos_l
        score = jnp.where(forced, FORCED_SCORE, jnp.where(valid, imp_t, -1.0))
        cnt = jnp.zeros((n_slc, tq), jnp.int32)
        for jp in range(n_slc):
            r = score[jp:jp + 1, :]
            beats = (r > score) | ((r == score) & (jb > jp))
            cnt = cnt + beats.astype(jnp.int32)
        self_t = jnp.where(cnt < n_sel, 0.0, NEG)
        if n_slc < LANES:
            self_t = jnp.concatenate([self_t, jnp.zeros((LANES - n_slc, tq), F32)], axis=0)
        selfeat = self_t.T.astype(BF16)
        qa = jnp.concatenate([qs, jnp.concatenate([selfeat] * NSA_GROUP, axis=0)], axis=1)

        tk = min(SLC_TK, t_len)
        n_kt = (i * tq + tq + tk - 1) // tk

        def kt_body(kt, carry):
            m_i, l_i, acc = carry
            ks = pl.ds(pl.multiple_of(kt * tk, tk), tk)
            s2 = _dot_nt(qa, ka_ref[ks, :])
            kpos = kt * tk + lax.broadcasted_iota(jnp.int32, (rows, tk), 1)
            s2 = jnp.where(kpos <= qpos_r, s2, NEG)
            m_n = jnp.maximum(m_i, jnp.max(s2, axis=-1, keepdims=True))
            al = jnp.exp(m_i - m_n)
            p2 = jnp.exp(s2 - m_n)
            l_n = al * l_i + jnp.sum(p2, axis=-1, keepdims=True)
            acc = al * acc + _dot(p2.astype(BF16), sv_ref[ks, :])
            return m_n, l_n, acc

        m0 = jnp.full((rows, 1), NEG, F32)
        l0 = jnp.zeros((rows, 1), F32)
        a0 = jnp.zeros((rows, LANES), F32)
        _, l_s, acc_s = lax.fori_loop(0, n_kt, kt_body, (m0, l0, a0))
        o_s = acc_s / l_s

        lw = min(WINDOW + tq, t_len)
        w0 = pl.multiple_of(jnp.maximum(i * tq + tq - lw, 0), tq)
        s3 = _dot_nt(qs, wk_ref[pl.ds(w0, lw), :])
        kpos = w0 + lax.broadcasted_iota(jnp.int32, (rows, lw), 1)
        rel = qpos_r - kpos
        msk = (rel >= 0) & (rel < WINDOW)
        s3 = jnp.where(msk, s3, NEG)
        m3 = jnp.max(s3, axis=-1, keepdims=True)
        p3 = jnp.where(msk, jnp.exp(s3 - m3), 0.0)
        den3 = jnp.sum(p3, axis=-1, keepdims=True)
        o_w = _dot(p3.astype(BF16), wv_ref[pl.ds(w0, lw), :]) / den3

        for br, ob in enumerate((o_c, o_s, o_w)):
            for gp in range(2):
                o_parts[br].append(_place_pair(ob[(2 * gp) * tq:(2 * gp + 1) * tq],
                                               ob[(2 * gp + 1) * tq:(2 * gp + 2) * tq], hp, lane))

    out = None
    for br in range(3):
        ob = jnp.concatenate(o_parts[br], axis=1)
        gexp = _dot3_gate(sig_gates, gex_ref[br])
        term = gexp * ob
        out = term if out is None else out + term
    o_ref[...] = (out * _silu(gate_ref[...])).astype(o_ref.dtype)


def _dot3_gate(sig, ex_bf):
    hi, mid, lo = _split3(sig)
    return _dot(hi, ex_bf) + _dot(mid, ex_bf) + _dot(lo, ex_bf)


def _nsa_prompt(q, kvb, kcv, g, b, t):
    assert t % TQ == 0 and t % SLC_L == 0 and (t <= SLC_TK or t % SLC_TK == 0)
    n_sub = t // CMP_D
    n_slc = t // SLC_L
    assert n_slc <= LANES
    q3 = q.reshape(b, t, D_NSA)
    kvb3 = kvb.reshape(b, t, 6 * KV_DIM)
    g3 = g.reshape(b, t, G_WIDTH)
    ovt = jnp.asarray(_overlap_t(n_slc, n_sub), BF16)
    gex = jnp.asarray(np.stack([_gate_expand(0), _gate_expand(1)]), BF16)
    pw = 2 * NSA_GROUP * NSA_DH

    def kvcol(which):
        return pl.BlockSpec((None, t, LANES), lambda bi, p, i: (bi, 0, 2 * which + p))

    return pl.pallas_call(
        _nsa_prompt_kernel,
        grid=(b, 2, t // TQ),
        in_specs=[pl.BlockSpec((None, TQ, pw), lambda bi, p, i: (bi, i, p)),
                  kvcol(2), kvcol(3), kvcol(4), kvcol(5),
                  pl.BlockSpec((None, None, n_sub, LANES), lambda bi, p, i: (bi, 0, 0, p)),
                  pl.BlockSpec((None, None, n_sub, LANES), lambda bi, p, i: (bi, 1, 0, p)),
                  pl.BlockSpec((None, TQ, pw), lambda bi, p, i: (bi, i, p)),
                  pl.BlockSpec((None, TQ, LANES), lambda bi, p, i: (bi, i, D_NSA // LANES)),
                  pl.BlockSpec(ovt.shape, lambda bi, p, i: (0, 0)),
                  pl.BlockSpec((None, 3, LANES, pw), lambda bi, p, i: (p, 0, 0, 0))],
        out_specs=pl.BlockSpec((None, TQ, pw), lambda bi, p, i: (bi, i, p)),
        out_shape=jax.ShapeDtypeStruct((b, t, D_NSA), BF16),
        scratch_shapes=[pltpu.VMEM((t, 2 * LANES), BF16)],
        compiler_params=_cparams(("arbitrary", "arbitrary", "arbitrary")),
        name="nsa_prompt",
    )(q3, kvb3, kvb3, kvb3, kvb3, kcv, kcv, g3, g3, ovt, gex)


CMP_PAGES = 32


def _cmp_bd_weights(cmp_w1, cmp_w2):
    w1 = cmp_w1.reshape(2, CMP_R, CMP_D, NSA_DH, CMP_HID)
    w1 = jnp.transpose(w1, (0, 2, 3, 1, 4)).reshape(2, CMP_D, NSA_DH, CMP_R * CMP_HID)
    eye2 = jnp.eye(2, dtype=w1.dtype)
    bd = jnp.einsum('wjdn,hg->wjhdgn', w1, eye2).reshape(2, CMP_D, LANES, 2 * CMP_R * CMP_HID)
    eye4 = jnp.eye(NSA_KV_HEADS, dtype=w1.dtype)
    w2q = jnp.einsum('wed,hg->whegd', cmp_w2, eye4).reshape(2, NSA_KV_HEADS, CMP_HID, KV_DIM)
    return bd.astype(BF16), w2q.astype(BF16)


def _cmp_sample_kernel(pt_ref, cache_ref, bd_ref, w2q_ref, pe_ref, w1f_ref, o_ref, buf_ref, acc_ref, sem_ref,
                       *, pages, n_b, n_c):
    w = pl.program_id(0)
    b = pl.program_id(1)
    c = pl.program_id(2)
    step = (w * n_b + b) * n_c + c
    total = 2 * n_b * n_c
    slot = step % 2
    rows = pages * PAGE_SIZE // CMP_D

    def copies(w_, b_, c_, slot_):
        out = []
        for p in range(pages):
            page = pt_ref[b_, c_ * pages + p]
            for half in range(2):
                out.append(pltpu.make_async_copy(
                    cache_ref.at[page, :, pl.ds(pl.multiple_of(w_ * KV_DIM + half * LANES, LANES), LANES)],
                    buf_ref.at[slot_, half, pl.ds(p * PAGE_SIZE, PAGE_SIZE), :],
                    sem_ref.at[slot_]))
        return out

    @pl.when(step == 0)
    def _():
        for cp in copies(w, b, c, slot):
            cp.start()

    @pl.when(step + 1 < total)
    def _():
        nxt = step + 1
        c_n = nxt % n_c
        b_n = (nxt // n_c) % n_b
        w_n = nxt // (n_c * n_b)
        for cp in copies(w_n, b_n, c_n, 1 - slot):
            cp.start()

    for cp in copies(w, b, c, slot):
        cp.wait()

    pw = 2 * CMP_R * CMP_HID
    for half in range(2):
        acc = jnp.zeros((rows, pw), F32)
        for j in range(CMP_D):
            xj = buf_ref.at[slot, half][pl.ds(j, rows, stride=CMP_D), :].astype(BF16)
            acc = acc + _dot(xj, bd_ref[j])
        acc_ref[pl.ds(pl.multiple_of(c * rows, rows), rows), half * pw:(half + 1) * pw] = acc

    @pl.when(c == n_c - 1)
    def _():
        n_sub = acc_ref.shape[0]
        pe_t = _pe_term(pe_ref, w1f_ref)
        out = jnp.zeros((n_sub, KV_DIM), F32)
        for h in range(NSA_KV_HEADS):
            c0 = h * CMP_R * CMP_HID
            nxt = pltpu.roll(acc_ref[:, c0 + CMP_HID:c0 + 2 * CMP_HID], n_sub - 1, 0)
            hid = pe_t + acc_ref[:, c0:c0 + CMP_HID] + nxt
            out = out + _dot(_silu(hid).astype(BF16), w2q_ref[h])
        o_ref[...] = out.astype(o_ref.dtype)


def _cmp_sample(cache2, page_table, bd, w2q, cmp_w1, cmp_pe):
    n_b, n_pages = page_table.shape
    pages = min(CMP_PAGES, n_pages)
    assert n_pages % pages == 0
    n_c = n_pages // pages
    n_sub = n_pages * PAGE_SIZE // CMP_D
    pe = cmp_pe.reshape(2, 1, CMP_L * NSA_DH)
    w1f = cmp_w1.reshape(2, CMP_L * NSA_DH, CMP_HID)
    nbd = NSA_KV_HEADS * CMP_R * CMP_HID
    kern = functools.partial(_cmp_sample_kernel, pages=pages, n_b=n_b, n_c=n_c)
    return pl.pallas_call(
        kern,
        grid_spec=pltpu.PrefetchScalarGridSpec(
            num_scalar_prefetch=1,
            grid=(2, n_b, n_c),
            in_specs=[pl.BlockSpec(memory_space=pl.ANY),
                      pl.BlockSpec((None, CMP_D, LANES, nbd // 2), lambda w, b, c, pt: (w, 0, 0, 0)),
                      pl.BlockSpec((None, NSA_KV_HEADS, CMP_HID, KV_DIM), lambda w, b, c, pt: (w, 0, 0, 0)),
                      pl.BlockSpec((None, 1, CMP_L * NSA_DH), lambda w, b, c, pt: (w, 0, 0)),
                      pl.BlockSpec((None, CMP_L * NSA_DH, CMP_HID), lambda w, b, c, pt: (w, 0, 0))],
            out_specs=pl.BlockSpec((None, None, n_sub, KV_DIM), lambda w, b, c, pt: (b, w, 0, 0)),
            scratch_shapes=[pltpu.VMEM((2, 2, pages * PAGE_SIZE, LANES), F32),
                            pltpu.VMEM((n_sub, nbd), F32),
                            pltpu.SemaphoreType.DMA((2,))]),
        out_shape=jax.ShapeDtypeStruct((n_b, 2, n_sub, KV_DIM), BF16),
        compiler_params=_cparams(("arbitrary", "arbitrary", "arbitrary")),
        name="cmp_sample",
    )(page_table, cache2, bd, w2q, pe, w1f)


def _nsa_sample_a_kernel(qh_ref, kc_ref, vc_ref, ov_ref, oc_ref, sel_ref, *, past_len, n_slc):
    n_sub = kc_ref.shape[0]
    n_slc_pad = ov_ref.shape[1]
    n_sel = min(N_SEL, n_slc)
    qpos = past_len
    rowi = lax.broadcasted_iota(jnp.int32, (SUBLANES, n_sub), 0)
    rowq = lax.broadcasted_iota(jnp.int32, (SROWS, n_sub), 0)
    ncol = lax.broadcasted_iota(jnp.int32, (SROWS, n_sub), 1)
    msk = (ncol * CMP_D + (CMP_L - 1) <= qpos) & (ncol < n_sub - 1)
    psum = jnp.zeros((SUBLANES, n_sub), F32)
    for h in range(NSA_KV_HEADS):
        pr = h // 2
        qh = (qh_ref[h] * NSA_DH ** -0.5).astype(BF16)
        s = _dot_nt(qh, kc_ref[:, pr * LANES:(pr + 1) * LANES])
        s = jnp.where(msk, s, NEG)
        m = jnp.max(s, axis=-1, keepdims=True)
        p = jnp.where(msk, jnp.exp(s - m), 0.0)
        den = jnp.sum(p, axis=-1, keepdims=True)
        p = p / jnp.where(den > 0, den, 1.0)
        oc_ref[h] = _dot(p.astype(BF16), vc_ref[:, pr * LANES:(pr + 1) * LANES])
        ph = jnp.sum(jnp.where(rowq < NSA_GROUP, p, 0.0), axis=0, keepdims=True)
        psum = jnp.where(rowi == h, ph, psum)

    hi, mid, lo = _split3(psum)
    imp = _dot(hi, ov_ref[...]) + _dot(mid, ov_ref[...]) + _dot(lo, ov_ref[...])
    jb = lax.broadcasted_iota(jnp.int32, (SUBLANES, n_slc_pad), 1)
    cur = qpos // SLC_L
    forced = (jb == 0) | (jb == cur) | (jb == cur - 1)
    valid = jb * SLC_L <= qpos
    score = jnp.where(forced, FORCED_SCORE, jnp.where(valid, imp, -1.0))
    score = jnp.where(jb < n_slc, score, -2.0)
    cnt = jnp.zeros((SUBLANES, n_slc_pad), jnp.int32)
    for jp in range(n_slc):
        r = score[:, jp:jp + 1]
        beats = (r > score) | ((r == score) & (jb > jp))
        cnt = cnt + beats.astype(jnp.int32)
    cnt = jnp.where(jb < n_slc, cnt, n_slc_pad)
    lane = lax.broadcasted_iota(jnp.int32, (SUBLANES, LANES), 1)
    idx = jnp.zeros((SUBLANES, LANES), jnp.int32)
    for slot in range(n_sel):
        pick = jnp.sum(jnp.where(cnt == slot, jb, 0), axis=1, keepdims=True)
        idx = jnp.where(lane == slot, pick, idx)
    sel_ref[...] = idx


def _nsa_sample_a(qh, kcv, past_len):
    n_b = qh.shape[0]
    n_sub = kcv.shape[2]
    n_slc = -(-(past_len + 1) // SLC_L)
    n_slc_pad = -(-n_slc // LANES) * LANES
    ov = np.zeros((n_sub, n_slc_pad), np.float32)
    ov[:, :n_slc] = _overlap_t(n_slc, n_sub).T
    ov = jnp.asarray(ov, BF16)
    kern = functools.partial(_nsa_sample_a_kernel, past_len=past_len, n_slc=n_slc)
    return pl.pallas_call(
        kern,
        grid=(n_b,),
        in_specs=[pl.BlockSpec((None, NSA_KV_HEADS, SROWS, LANES), lambda b: (b, 0, 0, 0)),
                  pl.BlockSpec((None, None, n_sub, KV_DIM), lambda b: (b, 0, 0, 0)),
                  pl.BlockSpec((None, None, n_sub, KV_DIM), lambda b: (b, 1, 0, 0)),
                  pl.BlockSpec(ov.shape, lambda b: (0, 0))],
        out_specs=[pl.BlockSpec((None, NSA_KV_HEADS, SROWS, LANES), lambda b: (b, 0, 0, 0)),
                   pl.BlockSpec((None, SUBLANES, LANES), lambda b: (b, 0, 0))],
        out_shape=[jax.ShapeDtypeStruct((n_b, NSA_KV_HEADS, SROWS, LANES), F32),
                   jax.ShapeDtypeStruct((n_b, SUBLANES, LANES), jnp.int32)],
        compiler_params=_cparams(("arbitrary",)),
        name="nsa_sample_a",
    )(qh, kcv, kcv, ov), n_slc


def _nsa_sample_b_kernel(pt_ref, sel_ref, qh_ref, cache_ref, new_ref, win_ref, oc_ref, gates_ref, gate_ref,
                         o_ref, kbuf_ref, vbuf_ref, sem_ref, *, past_len, n_slc, n_sel, n_b):
    b = pl.program_id(0)
    slot = b % 2
    n_blocks_cached = past_len // SLC_L
    per_page = PAGE_SIZE // SLC_L

    def copies(b_, slot_):
        out = []
        for h in range(NSA_KV_HEADS):
            for k in range(n_sel):
                j = jnp.minimum(sel_ref[b_, h, k], n_blocks_cached - 1)
                page = pt_ref[b_, j // per_page]
                off = pl.multiple_of((j % per_page) * SLC_L, SLC_L)
                for which, buf in ((2, kbuf_ref), (3, vbuf_ref)):
                    out.append(pltpu.make_async_copy(
                        cache_ref.at[page, pl.ds(off, SLC_L), pl.ds(which * KV_DIM + (h // 2) * LANES, LANES)],
                        buf.at[slot_, h, pl.ds(k * SLC_L, SLC_L), :],
                        sem_ref.at[slot_]))
        return out

    @pl.when(b == 0)
    def _():
        for cp in copies(b, slot):
            cp.start()

    @pl.when(b + 1 < n_b)
    def _():
        for cp in copies(b + 1, 1 - slot):
            cp.start()

    for cp in copies(b, slot):
        cp.wait()

    qpos = past_len
    nk = n_sel * SLC_L
    lane_k = lax.broadcasted_iota(jnp.int32, (1, nk), 1)
    n_buf = win_ref.shape[0]
    widx = lax.broadcasted_iota(jnp.int32, (1, n_buf), 1)
    rel = n_buf - widx
    wmask = (rel >= 0) & (rel < WINDOW) & (past_len - n_buf + widx >= 0)
    rowg = lax.broadcasted_iota(jnp.int32, (SROWS, LANES), 0)

    for h in range(NSA_KV_HEADS):
        pr = h // 2
        ls = slice(pr * LANES, (pr + 1) * LANES)
        qf = qh_ref[h] * NSA_DH ** -0.5
        qh = qf.astype(BF16)

        def new_key(row):
            return new_ref[row:row + 1, ls].astype(BF16).astype(F32)

        kpos = lane_k % SLC_L
        n_cur = jnp.int32(0)
        for k in range(n_sel):
            sj = sel_ref[b, h, k]
            kpos = kpos + jnp.where(lane_k // SLC_L == k, sj * SLC_L, 0)
            n_cur = n_cur + (sj == qpos // SLC_L).astype(jnp.int32)
        has_new = (jnp.zeros((1, 1), jnp.int32) + n_cur) > 0
        kvalid = kpos < past_len
        s = _dot_nt(qh, kbuf_ref[slot, h].astype(BF16))
        s = jnp.where(kvalid, s, NEG)
        s_new = jnp.sum(qf * new_key(2), axis=-1, keepdims=True)
        s_new = jnp.where(has_new, s_new, NEG)
        m = jnp.maximum(jnp.max(s, axis=-1, keepdims=True), s_new)
        p = jnp.where(kvalid, jnp.exp(s - m), 0.0)
        p_new = jnp.where(has_new, jnp.exp(s_new - m), 0.0)
        den = jnp.sum(p, axis=-1, keepdims=True) + p_new
        den = jnp.where(den > 0, den, 1.0)
        o_s = (_dot(p.astype(BF16), vbuf_ref[slot, h].astype(BF16)) + p_new * new_key(3)) / den

        s = _dot_nt(qh, win_ref[:, ls].astype(BF16))
        s = jnp.where(wmask, s, NEG)
        s_new = jnp.sum(qf * new_key(4), axis=-1, keepdims=True)
        m = jnp.maximum(jnp.max(s, axis=-1, keepdims=True), s_new)
        p = jnp.where(wmask, jnp.exp(s - m), 0.0)
        p_new = jnp.exp(s_new - m)
        den = jnp.sum(p, axis=-1, keepdims=True) + p_new
        wv = win_ref[:, KV_DIM + pr * LANES:KV_DIM + (pr + 1) * LANES].astype(BF16)
        o_w = (_dot(p.astype(BF16), wv) + p_new * new_key(5)) / den

        g = jax.nn.sigmoid(gates_ref[:, h])
        o = g[0] * oc_ref[h] + g[1] * o_s + g[2] * o_w
        o_ref[h] = jnp.where(rowg < NSA_GROUP, o * _silu(gate_ref[h]), 0.0)


def _nsa_sample_b(page_table, sel, qh, cache2, kv_new, win2, o_c, gates_l, gate_l, past_len, n_slc):
    n_b = qh.shape[0]
    n_sel = sel.shape[2]
    n_buf = win2.shape[1]
    kern = functools.partial(_nsa_sample_b_kernel, past_len=past_len, n_slc=n_slc, n_sel=n_sel, n_b=n_b)
    hb = (None, NSA_KV_HEADS, SROWS, LANES)
    return pl.pallas_call(
        kern,
        grid_spec=pltpu.PrefetchScalarGridSpec(
            num_scalar_prefetch=2,
            grid=(n_b,),
            in_specs=[pl.BlockSpec(hb, lambda b, pt, sl: (b, 0, 0, 0)),
                      pl.BlockSpec(memory_space=pl.ANY),
                      pl.BlockSpec((None, 6, KV_DIM), lambda b, pt, sl: (b, 0, 0)),
                      pl.BlockSpec((None, n_buf, 2 * KV_DIM), lambda b, pt, sl: (b, 0, 0)),
                      pl.BlockSpec(hb, lambda b, pt, sl: (b, 0, 0, 0)),
                      pl.BlockSpec((None, 3, NSA_KV_HEADS, SROWS, LANES), lambda b, pt, sl: (b, 0, 0, 0, 0)),
                      pl.BlockSpec(hb, lambda b, pt, sl: (b, 0, 0, 0))],
            out_specs=pl.BlockSpec(hb, lambda b, pt, sl: (b, 0, 0, 0)),
            scratch_shapes=[pltpu.VMEM((2, NSA_KV_HEADS, n_sel * SLC_L, LANES), F32),
                            pltpu.VMEM((2, NSA_KV_HEADS, n_sel * SLC_L, LANES), F32),
                            pltpu.SemaphoreType.DMA((2,))]),
        out_shape=jax.ShapeDtypeStruct((n_b, NSA_KV_HEADS, SROWS, LANES), F32),
        compiler_params=_cparams(("arbitrary",)),
        name="nsa_sample_b",
    )(page_table, sel, qh, cache2, kv_new, win2, o_c, gates_l, gate_l)


def _head_layout(x):
    n_b = x.shape[0]
    x = x.astype(F32)
    z = jnp.zeros_like(x)
    even = jnp.concatenate([x, z], axis=-1)
    odd = jnp.concatenate([z, x], axis=-1)
    par = (jnp.arange(NSA_KV_HEADS) % 2).reshape(1, NSA_KV_HEADS, 1, 1)
    y = jnp.where(par == 0, even, odd)
    return jnp.concatenate([y, jnp.zeros((n_b, NSA_KV_HEADS, SROWS - NSA_GROUP, LANES), F32)], axis=2)


def _sample_path(x_sample, cache_kv, cache_win, state, page_table, w_p, w_out_b, hg_lb, norm_g,
                 cmp_w1, cmp_w2, cmp_pe, ln_g, ln_b):
    n_b, t_s, _ = x_sample.shape
    assert t_s == 1, "single-token decode"
    n_pool = cache_kv.shape[0]
    n_pages = page_table.shape[1]
    past_len = n_pages * PAGE_SIZE
    xs = x_sample.reshape(n_b, D_MODEL)
    hg, q, kv, _, g = _inproj(xs, w_p, n_b)
    o_hg, s_new = _hgrn_sample(hg, state, hg_lb, norm_g)

    cache2 = cache_kv.reshape(n_pool, PAGE_SIZE, 4 * KV_DIM)
    bd, w2q = _cmp_bd_weights(cmp_w1, cmp_w2)
    kcv = _cmp_sample(cache2, page_table, bd, w2q, cmp_w1, cmp_pe)

    qh = _head_layout(q.reshape(n_b, NSA_KV_HEADS, NSA_GROUP, NSA_DH))
    (o_c, sel), n_slc = _nsa_sample_a(qh, kcv, past_len)
    n_sel = min(N_SEL, n_slc)
    sel = sel[:, :NSA_KV_HEADS, :n_sel]

    gates = g[:, D_NSA:D_NSA + 3 * NSA_HEADS].reshape(n_b, 3, NSA_KV_HEADS, NSA_GROUP, 1)
    gates_l = _head_layout_b(jnp.broadcast_to(gates, (n_b, 3, NSA_KV_HEADS, NSA_GROUP, NSA_DH)))
    gate_l = _head_layout(g[:, :D_NSA].reshape(n_b, NSA_KV_HEADS, NSA_GROUP, NSA_DH))
    n_buf = cache_win.shape[1]
    win2 = cache_win.reshape(n_b, n_buf, 2 * KV_DIM)
    o = _nsa_sample_b(page_table, sel, qh, cache2, kv.reshape(n_b, 6, KV_DIM), win2, o_c, gates_l, gate_l,
                      past_len, n_slc)
    o = o.reshape(n_b, NSA_KV_HEADS, SROWS, 2, NSA_DH)[:, :, :NSA_GROUP].sum(axis=3)
    o_nsa = o.reshape(n_b, D_NSA).astype(BF16)

    y = _outproj(xs, o_hg.reshape(n_b, D_HG).astype(BF16), o_nsa, w_out_b, ln_g, ln_b, n_b)
    kv6 = kv.reshape(n_b, 1, 6, NSA_KV_HEADS, NSA_DH)
    win_cat = jnp.concatenate([cache_win, kv6[:, :, 4:].astype(cache_win.dtype)], axis=1)
    return (y.reshape(n_b, 1, D_MODEL), kv6[None, :, :, :4].astype(cache_kv.dtype), win_cat[None, :, 1:],
            s_new[None].astype(state.dtype))


def _head_layout_b(x):
    n_b = x.shape[0]
    y = _head_layout(x.reshape(n_b * 3, NSA_KV_HEADS, NSA_GROUP, NSA_DH))
    return y.reshape(n_b, 3, NSA_KV_HEADS, SROWS, LANES)


def kernel(x_prompt, x_sample, cache_kv, cache_win, state_hgrn, page_table, w_in, hg_lb, hg_norm_g,
           cmp_w1, cmp_w2, cmp_pe, w_out, ln_g, ln_b):
    assert w_in.shape[0] == 1, "single-layer decoder"
    b, t, _ = x_prompt.shape
    w_p = _prep_w_in(w_in[0])
    w_out_b = w_out[0].astype(BF16)
    w1p, w2p = _cmp_weights(cmp_w1[0], cmp_w2[0])

    xp = x_prompt.reshape(b * t, D_MODEL)
    tm = 1024 if (b * t) % 1024 == 0 else 256
    hg, q, kv, kvb, g = _inproj(xp, w_p, tm)
    o_hg, s_p = _hgrn_prompt(hg, hg_lb, hg_norm_g[0], b, t)
    kcv = _cmp_prompt(kv, w1p, w2p, cmp_w1[0], cmp_pe[0], b, t)
    o_nsa = _nsa_prompt(q, kvb, kcv, g, b, t)
    y_p = _outproj(xp, o_hg.reshape(b * t, D_HG), o_nsa.reshape(b * t, D_NSA), w_out_b,
                   ln_g[0], ln_b[0], 512 if (b * t) % 512 == 0 else 256)
    kv6 = kv.reshape(b, t, 6, NSA_KV_HEADS, NSA_DH)
    wlen = min(WINDOW, t)

    y_s, kv_s, win_s, s_s = _sample_path(x_sample, cache_kv[0], cache_win[0], state_hgrn[0], page_table, w_p,
                                         w_out_b, hg_lb, hg_norm_g[0], cmp_w1[0], cmp_w2[0], cmp_pe[0],
                                         ln_g[0], ln_b[0])
    return (y_p.reshape(b, t, D_MODEL), y_s, kv6[None, :, :, :4], kv6[None, :, t - wlen:, 4:],
            s_p[None].astype(x_prompt.dtype), kv_s, win_s, s_s)
```

```python
import functools

import numpy as np
import jax
import jax.numpy as jnp
from jax import lax
from jax.experimental import pallas as pl
from jax.experimental.pallas import tpu as pltpu

F32 = jnp.float32
BF16 = jnp.bfloat16

D_MODEL = 2048
D_HG = 1024
D_NSA = 1024
HG_DK = 128
HG_DV = 128
HG_HEADS = 8
NSA_DH = 64
NSA_HEADS = 16
NSA_KV_HEADS = 4
NSA_GROUP = 4
KV_DIM = 256
CMP_L = 32
CMP_D = 16
CMP_R = 2
CMP_HID = 128
SLC_L = 64
N_SEL = 16
WINDOW = 512
PAGE_SIZE = 128
ALPHA = 2.0 ** 0.25
LN_EPS = 1e-5
RMS_EPS = 1e-6
FORCED_SCORE = 1e4
NEG = -1e30

LANES = 128
SUBLANES = 8
VMEM_LIMIT = 56 * 1024 * 1024
SROWS = 16

COL_TILE = 512
N_HG_TILES = 4 * D_HG // COL_TILE
N_Q_TILES = D_NSA // COL_TILE
N_KV_TILES = 6 * KV_DIM // COL_TILE
G_WIDTH = 1536
N_G_TILES = G_WIDTH // COL_TILE
N_COL_TILES = N_HG_TILES + N_Q_TILES + N_KV_TILES + N_G_TILES
D_IN_PAD = N_COL_TILES * COL_TILE

HG_CHUNK = 128
HG_LEVELS = 4


def _cparams(sem, flags=None):
    return pltpu.CompilerParams(dimension_semantics=sem, vmem_limit_bytes=VMEM_LIMIT, flags=flags)


def _dot(a, b):
    return jnp.dot(a, b, preferred_element_type=F32)


def _dot_nt(a, b):
    return lax.dot_general(a, b, (((1,), (1,)), ((), ())), preferred_element_type=F32)


def _split3(x):
    hi = x.astype(BF16)
    r1 = x - hi.astype(F32)
    mid = r1.astype(BF16)
    lo = (r1 - mid.astype(F32)).astype(BF16)
    return hi, mid, lo


def _dot3(a_bf, x):
    hi, mid, lo = _split3(x)
    return _dot(a_bf, hi) + _dot(a_bf, mid) + _dot(a_bf, lo)


def _dot3_nt(a_bf, x):
    hi, mid, lo = _split3(x)
    return _dot_nt(a_bf, hi) + _dot_nt(a_bf, mid) + _dot_nt(a_bf, lo)


def _silu(x):
    return x * jax.nn.sigmoid(x)


def _inproj_kernel(x_ref, w_ref, hg_ref, q_ref, kv_ref, kvb_ref, g_ref, xb_ref):
    j = pl.program_id(1)

    @pl.when(j == 0)
    def _():
        xb_ref[...] = x_ref[...].astype(BF16)

    acc = _dot(xb_ref[...], w_ref[...])
    q0 = N_HG_TILES
    kv0 = q0 + N_Q_TILES
    g0 = kv0 + N_KV_TILES

    @pl.when(j < q0)
    def _():
        hg_ref[...] = acc

    @pl.when((j >= q0) & (j < kv0))
    def _():
        q_ref[...] = acc.astype(BF16)

    @pl.when((j >= kv0) & (j < g0))
    def _():
        kv_ref[...] = acc
        kvb_ref[...] = acc.astype(BF16)

    @pl.when(j >= g0)
    def _():
        g_ref[...] = acc


def _inproj(x2d, w_p, tm):
    m = x2d.shape[0]
    assert m % tm == 0
    q0 = N_HG_TILES
    kv0 = q0 + N_Q_TILES
    g0 = kv0 + N_KV_TILES

    def clampmap(lo, n):
        return lambda i, j: (i, jnp.clip(j - lo, 0, n - 1))

    return pl.pallas_call(
        _inproj_kernel,
        grid=(m // tm, N_COL_TILES),
        in_specs=[pl.BlockSpec((tm, D_MODEL), lambda i, j: (i, 0)),
                  pl.BlockSpec((D_MODEL, COL_TILE), lambda i, j: (0, j))],
        out_specs=[pl.BlockSpec((tm, COL_TILE), clampmap(0, N_HG_TILES)),
                   pl.BlockSpec((tm, COL_TILE), clampmap(q0, N_Q_TILES)),
                   pl.BlockSpec((tm, COL_TILE), clampmap(kv0, N_KV_TILES)),
                   pl.BlockSpec((tm, COL_TILE), clampmap(kv0, N_KV_TILES)),
                   pl.BlockSpec((tm, COL_TILE), clampmap(g0, N_G_TILES))],
        out_shape=[jax.ShapeDtypeStruct((m, 4 * D_HG), F32),
                   jax.ShapeDtypeStruct((m, D_NSA), BF16),
                   jax.ShapeDtypeStruct((m, 6 * KV_DIM), F32),
                   jax.ShapeDtypeStruct((m, 6 * KV_DIM), BF16),
                   jax.ShapeDtypeStruct((m, G_WIDTH), F32)],
        scratch_shapes=[pltpu.VMEM((tm, D_MODEL), BF16)],
        compiler_params=_cparams(("arbitrary", "arbitrary")),
        name="inproj",
    )(x2d, w_p)


def _prep_w_in(w_in):
    o_nq = 4 * D_HG
    o_kv = o_nq + D_NSA
    o_gates = o_kv + 6 * KV_DIM
    o_gate = o_gates + 3 * NSA_HEADS
    pad = G_WIDTH - D_NSA - 3 * NSA_HEADS
    w = jnp.concatenate([w_in[:, :o_gates], w_in[:, o_gate:], w_in[:, o_gates:o_gate],
                         jnp.zeros((D_MODEL, pad), w_in.dtype)], axis=1)
    return w.astype(BF16)


def _outproj_kernel(x_ref, ohg_ref, onsa_ref, w_ref, g_ref, b_ref, y_ref):
    y = _dot(ohg_ref[...], w_ref[0:D_HG, :]) + _dot(onsa_ref[...], w_ref[D_HG:D_MODEL, :])
    z = ALPHA * x_ref[...] + y
    mu = jnp.mean(z, axis=-1, keepdims=True)
    zc = z - mu
    var = jnp.mean(zc * zc, axis=-1, keepdims=True)
    y_ref[...] = zc * lax.rsqrt(var + LN_EPS) * g_ref[...] + b_ref[...]


def _outproj(x2d, o_hg, o_nsa, w_out_b, ln_g, ln_b, tm):
    m = x2d.shape[0]
    assert m % tm == 0
    return pl.pallas_call(
        _outproj_kernel,
        grid=(m // tm,),
        in_specs=[pl.BlockSpec((tm, D_MODEL), lambda i: (i, 0)),
                  pl.BlockSpec((tm, D_HG), lambda i: (i, 0)),
                  pl.BlockSpec((tm, D_NSA), lambda i: (i, 0)),
                  pl.BlockSpec((D_MODEL, D_MODEL), lambda i: (0, 0)),
                  pl.BlockSpec((1, D_MODEL), lambda i: (0, 0)),
                  pl.BlockSpec((1, D_MODEL), lambda i: (0, 0))],
        out_specs=pl.BlockSpec((tm, D_MODEL), lambda i: (i, 0)),
        out_shape=jax.ShapeDtypeStruct((m, D_MODEL), F32),
        compiler_params=_cparams(("arbitrary",)),
        name="outproj_ln",
    )(x2d, o_hg, o_nsa, w_out_b, ln_g.reshape(1, D_MODEL), ln_b.reshape(1, D_MODEL))


def _hgrn_gmat():
    c = HG_CHUNK
    tri = np.tril(np.ones((c, c), np.float32))
    mats = [tri]
    for lvl in range(1, HG_LEVELS + 1):
        half = SUBLANES * 2 ** (lvl - 1)
        t = np.arange(c)
        mid = (t // (2 * half)) * (2 * half) + half
        mats.append(tri - tri[mid - 1])
    return jnp.asarray(np.concatenate(mats, axis=0), BF16)


def _hgrn_lower_bound(lb_ref):
    raw = lb_ref[...]
    mx = jnp.max(raw, axis=0, keepdims=True)
    e = jnp.exp(raw - mx)
    return e[0:1, :] / jnp.sum(e, axis=0, keepdims=True)


def _hgrn_kernel(lb_ref, ng_ref, gm_ref, q_ref, f_ref, i_ref, g_ref, o_ref, s_ref, st_ref):
    c = HG_CHUNK
    t_len = q_ref.shape[0]
    lb = _hgrn_lower_bound(lb_ref)
    ng = ng_ref[...]
    st_ref[...] = jnp.zeros_like(st_ref)

    row = lax.broadcasted_iota(jnp.int32, (c, HG_DK), 0)
    sub3 = lax.broadcasted_iota(jnp.int32, (c // SUBLANES, SUBLANES, HG_DK), 1)
    r2 = lax.broadcasted_iota(jnp.int32, (c, c), 0)
    c2 = lax.broadcasted_iota(jnp.int32, (c, c), 1)

    def chunk(ci, carry):
        sl = pl.ds(pl.multiple_of(ci * c, c), c)
        fr = f_ref[sl, :]
        q = q_ref[sl, :]
        v = i_ref[sl, :]
        logf = jnp.log(lb + (1.0 - lb) * jax.nn.sigmoid(fr))
        k = (1.0 - lb) * jax.nn.sigmoid(-fr)
        dall = _dot3(gm_ref[...], logf)
        a = dall[0:c]

        q3 = q.reshape(c // SUBLANES, SUBLANES, HG_DK)
        k3 = k.reshape(c // SUBLANES, SUBLANES, HG_DK)
        v3 = v.reshape(c // SUBLANES, SUBLANES, HG_DV)
        a3 = a.reshape(c // SUBLANES, SUBLANES, HG_DK)
        od = jnp.sum(q3 * k3, axis=-1, keepdims=True) * v3
        for d in range(1, SUBLANES):
            ks = pltpu.roll(k3, d, 1)
            as_ = pltpu.roll(a3, d, 1)
            vs = pltpu.roll(v3, d, 1)
            ok = sub3 >= d
            w = jnp.where(ok, q3 * ks * jnp.exp(jnp.minimum(a3 - as_, 0.0)), 0.0)
            od = od + jnp.sum(w, axis=-1, keepdims=True) * vs
        od = od.reshape(c, HG_DV)

        att = jnp.zeros((c, c), F32)
        for lvl in range(1, HG_LEVELS + 1):
            half = SUBLANES * 2 ** (lvl - 1)
            dl = dall[lvl * c:(lvl + 1) * c]
            e = jnp.exp(-jnp.abs(dl))
            upper = (row % (2 * half)) >= half
            qt = jnp.where(upper, q * e, 0.0).astype(BF16)
            kt = jnp.where(upper, 0.0, k * e).astype(BF16)
            al = _dot_nt(qt, kt)
            if 2 * half < c:
                al = jnp.where((r2 // (2 * half)) == (c2 // (2 * half)), al, 0.0)
            att = att + al
        vb = v.astype(BF16)
        o = od + _dot(att.astype(BF16), vb)

        st = st_ref[...]
        o = o + _dot_nt((q * jnp.exp(a)).astype(BF16), st.astype(BF16))
        alast = a[c - 1:c, :]
        kd = (k * jnp.exp(alast - a)).astype(BF16)
        st_ref[...] = st * jnp.exp(alast) + _dot(v.T.astype(BF16), kd)

        o = o * lax.rsqrt(jnp.mean(o * o, axis=-1, keepdims=True) + RMS_EPS)
        o_ref[sl, :] = (o * ng * _silu(g_ref[sl, :])).astype(o_ref.dtype)
        return carry

    n_chunks = t_len // c
    lax.fori_loop(0, n_chunks, chunk, 0, unroll=2 if n_chunks % 2 == 0 else 1)
    s_ref[...] = st_ref[...].T


def _hgrn_prompt(hg, hg_lb, norm_g, b, t):
    assert t % HG_CHUNK == 0
    hg3 = hg.reshape(b, t, 4 * D_HG)
    gm = _hgrn_gmat()

    def col(k):
        return pl.BlockSpec((None, t, HG_DK), lambda bi, h: (bi, 0, k * HG_HEADS + h))

    return pl.pallas_call(
        _hgrn_kernel,
        grid=(b, HG_HEADS),
        in_specs=[pl.BlockSpec((2, HG_DK), lambda bi, h: (0, h)),
                  pl.BlockSpec((1, HG_DV), lambda bi, h: (0, h)),
                  pl.BlockSpec(gm.shape, lambda bi, h: (0, 0)),
                  col(0), col(1), col(2), col(3)],
        out_specs=[pl.BlockSpec((None, t, HG_DV), lambda bi, h: (bi, 0, h)),
                   pl.BlockSpec((None, None, HG_DK, HG_DV), lambda bi, h: (bi, h, 0, 0))],
        out_shape=[jax.ShapeDtypeStruct((b, t, D_HG), BF16),
                   jax.ShapeDtypeStruct((b, HG_HEADS, HG_DK, HG_DV), F32)],
        scratch_shapes=[pltpu.VMEM((HG_DV, HG_DK), F32)],
        compiler_params=_cparams(("arbitrary", "arbitrary")),
        name="hgrn_prompt",
    )(hg_lb, norm_g.reshape(1, D_HG), gm, hg3, hg3, hg3, hg3)


def _hgrn_step_kernel(lb_ref, ng_ref, x_ref, s_ref, o_ref, so_ref):
    raw = lb_ref[...]
    mx = jnp.max(raw, axis=0)
    e0 = jnp.exp(raw[0] - mx)
    lb_all = e0 / (e0 + jnp.exp(raw[1] - mx))
    x = x_ref[...]
    eye = (lax.broadcasted_iota(jnp.int32, (HG_DK, HG_DK), 0)
           == lax.broadcasted_iota(jnp.int32, (HG_DK, HG_DK), 1))

    def column(rowvec):
        return jnp.sum(jnp.where(eye, rowvec, 0.0), axis=1, keepdims=True)

    for h in range(HG_HEADS):
        lb = lb_all[h:h + 1]
        q = x[h:h + 1]
        fr = x[HG_HEADS + h:HG_HEADS + h + 1]
        v = x[2 * HG_HEADS + h:2 * HG_HEADS + h + 1]
        gate = x[3 * HG_HEADS + h:3 * HG_HEADS + h + 1]
        f = lb + (1.0 - lb) * jax.nn.sigmoid(fr)
        k = (1.0 - lb) * jax.nn.sigmoid(-fr)
        s_new = s_ref[h] * column(f) + column(k) * v
        so_ref[h] = s_new
        o = jnp.sum(column(q) * s_new, axis=0, keepdims=True)
        o = o * lax.rsqrt(jnp.mean(o * o, axis=-1, keepdims=True) + RMS_EPS)
        o_ref[h:h + 1, :] = o * ng_ref[h:h + 1, :] * _silu(gate)


def _hgrn_sample(hg_s, state, hg_lb, norm_g):
    b = hg_s.shape[0]
    x3 = hg_s.reshape(b, 4 * HG_HEADS, HG_DK)
    return pl.pallas_call(
        _hgrn_step_kernel,
        grid=(b,),
        in_specs=[pl.BlockSpec((2, HG_HEADS, HG_DK), lambda i: (0, 0, 0)),
                  pl.BlockSpec((HG_HEADS, HG_DV), lambda i: (0, 0)),
                  pl.BlockSpec((None, 4 * HG_HEADS, HG_DK), lambda i: (i, 0, 0)),
                  pl.BlockSpec((None, HG_HEADS, HG_DK, HG_DV), lambda i: (i, 0, 0, 0))],
        out_specs=[pl.BlockSpec((None, HG_HEADS, HG_DV), lambda i: (i, 0, 0)),
                   pl.BlockSpec((None, HG_HEADS, HG_DK, HG_DV), lambda i: (i, 0, 0, 0))],
        out_shape=[jax.ShapeDtypeStruct((b, HG_HEADS, HG_DV), F32),
                   jax.ShapeDtypeStruct((b, HG_HEADS, HG_DK, HG_DV), F32)],
        compiler_params=_cparams(("arbitrary",)),
        name="hgrn_sample",
    )(hg_lb.reshape(2, HG_HEADS, HG_DK), norm_g.reshape(HG_HEADS, HG_DV), x3, state)


def _cmp_weights(cmp_w1, cmp_w2):
    w1 = cmp_w1.reshape(2, CMP_R, CMP_D, NSA_DH, CMP_HID)
    w1 = jnp.transpose(w1, (0, 2, 3, 1, 4)).reshape(2, CMP_D, NSA_DH, CMP_R * CMP_HID)
    z1 = jnp.zeros_like(w1)
    w1p = jnp.stack([jnp.concatenate([w1, z1], axis=2), jnp.concatenate([z1, w1], axis=2)], axis=1)
    z2 = jnp.zeros_like(cmp_w2)
    w2p = jnp.stack([jnp.concatenate([cmp_w2, z2], axis=2), jnp.concatenate([z2, cmp_w2], axis=2)], axis=1)
    return w1p.astype(BF16), w2p.astype(BF16)


def _pe_term(pe_ref, w1f_ref):
    pe = jnp.broadcast_to(pe_ref[...], (SUBLANES, CMP_L * NSA_DH))
    w = w1f_ref[...]
    wh, wm, wl = _split3(w)
    ph, pm, pl_ = _split3(pe)
    out = (_dot(ph, wh) + _dot(ph, wm) + _dot(pm, wh)
           + _dot(ph, wl) + _dot(pm, wm) + _dot(pl_, wh))
    return out[0:1, :]


def _cmp_prompt_kernel(x_ref, w1p_ref, w2p_ref, pe_ref, w1f_ref, o_ref):
    n_sub = x_ref.shape[0] // CMP_D
    pe_t = _pe_term(pe_ref, w1f_ref)
    out = jnp.zeros((n_sub, LANES), F32)
    for par in range(2):
        acc = jnp.zeros((n_sub, CMP_R * CMP_HID), F32)
        for j in range(CMP_D):
            xj = x_ref[pl.ds(j, n_sub, stride=CMP_D), :].astype(BF16)
            acc = acc + _dot(xj, w1p_ref[par, j])
        nxt = pltpu.roll(acc[:, CMP_HID:], n_sub - 1, 0)
        hid = pe_t + acc[:, :CMP_HID] + nxt
        out = out + _dot(_silu(hid).astype(BF16), w2p_ref[par])
    o_ref[...] = out.astype(o_ref.dtype)


def _cmp_prompt(kv, w1p, w2p, cmp_w1, cmp_pe, b, t):
    assert t % CMP_D == 0 and (t // CMP_D) % SUBLANES == 0
    n_sub = t // CMP_D
    kv3 = kv.reshape(b, t, 6 * KV_DIM)
    pe = cmp_pe.reshape(2, 1, CMP_L * NSA_DH)
    w1f = cmp_w1.reshape(2, CMP_L * NSA_DH, CMP_HID)
    return pl.pallas_call(
        _cmp_prompt_kernel,
        grid=(b, 2, 2),
        in_specs=[pl.BlockSpec((None, t, LANES), lambda bi, w, p: (bi, 0, 2 * w + p)),
                  pl.BlockSpec((None, 2, CMP_D, LANES, CMP_R * CMP_HID), lambda bi, w, p: (w, 0, 0, 0, 0)),
                  pl.BlockSpec((None, 2, CMP_HID, LANES), lambda bi, w, p: (w, 0, 0, 0)),
                  pl.BlockSpec((None, 1, CMP_L * NSA_DH), lambda bi, w, p: (w, 0, 0)),
                  pl.BlockSpec((None, CMP_L * NSA_DH, CMP_HID), lambda bi, w, p: (w, 0, 0))],
        out_specs=pl.BlockSpec((None, None, n_sub, LANES), lambda bi, w, p: (bi, w, 0, p)),
        out_shape=jax.ShapeDtypeStruct((b, 2, n_sub, KV_DIM), BF16),
        compiler_params=_cparams(("arbitrary", "arbitrary", "arbitrary")),
        name="cmp_prompt",
    )(kv3, w1p, w2p, pe, w1f)


TQ = 256
SLC_TK = 512


def _overlap_t(n_slc, n_cmp_pad):
    ci = np.arange(n_cmp_pad)[None, :] * CMP_D
    sj = np.arange(n_slc)[:, None] * SLC_L
    return ((ci < sj + SLC_L) & (ci + CMP_L > sj)).astype(np.float32)


def _gate_expand(pair):
    m = np.zeros((3, LANES, 2 * NSA_GROUP * NSA_DH), np.float32)
    for br in range(3):
        for hp in range(2):
            for g in range(NSA_GROUP):
                src = br * NSA_HEADS + (2 * pair + hp) * NSA_GROUP + g
                c0 = (hp * NSA_GROUP + g) * NSA_DH
                m[br, src, c0:c0 + NSA_DH] = 1.0
    return m


def _place_pair(x_even, x_odd, hp, lane):
    if hp == 0:
        return jnp.where(lane < NSA_DH, x_even, pltpu.roll(x_odd, NSA_DH, 1))
    return jnp.where(lane < NSA_DH, pltpu.roll(x_even, NSA_DH, 1), x_odd)


def _nsa_prompt_kernel(q_ref, sk_ref, sv_ref, wk_ref, wv_ref, kc_ref, vc_ref, gate_ref, gates_ref,
                       ovt_ref, gex_ref, o_ref, ka_ref, va_ref, wka_ref, wva_ref):
    i = pl.program_id(2)
    tq = q_ref.shape[0]
    t_len = sk_ref.shape[0]
    n_slc = ovt_ref.shape[0]
    n_cmp_pad = kc_ref.shape[0]
    rows = 2 * NSA_GROUP * tq
    n_sel = min(N_SEL, n_slc)

    @pl.when(i == 0)
    def _():
        ones = jnp.ones((t_len, LANES), BF16)
        jl = lax.broadcasted_iota(jnp.int32, (t_len, LANES), 1)
        ka_ref[:, 0:LANES] = sk_ref[...]
        kb = lax.broadcasted_iota(jnp.int32, (t_len, LANES), 0) // SLC_L
        ka_ref[:, LANES:2 * LANES] = jnp.where(kb == jl, 1.0, 0.0).astype(BF16)
        va_ref[0] = jnp.where(jl < NSA_DH, sv_ref[...], ones)
        va_ref[1] = jnp.where(jl < NSA_DH, ones, sv_ref[...])
        jp = lax.broadcasted_iota(jnp.int32, (WINDOW, 2 * LANES), 1)
        wka_ref[0:WINDOW, :] = jnp.where(jp == LANES, NEG, 0.0).astype(BF16)
        wka_ref[WINDOW:WINDOW + t_len, 0:LANES] = wk_ref[...]
        wka_ref[WINDOW:WINDOW + t_len, LANES:2 * LANES] = jnp.zeros((t_len, LANES), BF16)
        for hp in range(2):
            wva_ref[hp, 0:WINDOW, :] = jnp.zeros((WINDOW, LANES), BF16)
        wva_ref[0, WINDOW:WINDOW + t_len, :] = jnp.where(jl < NSA_DH, wv_ref[...], ones)
        wva_ref[1, WINDOW:WINDOW + t_len, :] = jnp.where(jl < NSA_DH, ones, wv_ref[...])

    lane = lax.broadcasted_iota(jnp.int32, (tq, LANES), 1)
    qpos_r = i * tq + lax.broadcasted_iota(jnp.int32, (rows, 1), 0) % tq
    sig_gates = jax.nn.sigmoid(gates_ref[...])

    qs = []
    for hp in range(2):
        keep = (lane >= NSA_DH) if hp == 1 else (lane < NSA_DH)
        for g in range(NSA_GROUP):
            blk = hp * 2 + g // 2
            x = q_ref[:, blk * LANES:(blk + 1) * LANES]
            if g % 2 != hp:
                x = pltpu.roll(x, NSA_DH, 1)
            qs.append(jnp.where(keep, x, jnp.zeros_like(x)))
    qs = jnp.concatenate(qs, axis=0) * jnp.asarray(NSA_DH ** -0.5, BF16)

    s = _dot_nt(qs, kc_ref[...])
    end = lax.broadcasted_iota(jnp.int32, (rows, n_cmp_pad), 1) * CMP_D + (CMP_L - 1)
    msk = end <= qpos_r
    s = jnp.where(msk, s, NEG)
    m = jnp.max(s, axis=-1, keepdims=True)
    p = jnp.where(msk, jnp.exp(s - m), 0.0)
    den = jnp.sum(p, axis=-1, keepdims=True)
    p = p / jnp.where(den > 0, den, 1.0)
    o_c = _dot(p.astype(BF16), vc_ref[...])

    jb = lax.broadcasted_iota(jnp.int32, (n_slc, tq), 0)
    qpos_l = i * tq + lax.broadcasted_iota(jnp.int32, (n_slc, tq), 1)
    cur = qpos_l // SLC_L
    forced = (jb == 0) | (jb == cur) | (jb == cur - 1)
    valid = jb * SLC_L <= qpos_l
    selfeat = []
    for hp in range(2):
        r0 = hp * NSA_GROUP * tq
        psum = p[r0:r0 + tq] + p[r0 + tq:r0 + 2 * tq] + p[r0 + 2 * tq:r0 + 3 * tq] + p[r0 + 3 * tq:r0 + 4 * tq]
        imp_t = _dot3_nt(ovt_ref[...], psum)
        score = jnp.where(forced, FORCED_SCORE, jnp.where(valid, imp_t, -1.0))
        cnt = jnp.zeros((n_slc, tq), jnp.int32)
        for jp in range(n_slc):
            r = score[jp:jp + 1, :]
            beats = (r > score) | ((r == score) & (jb > jp))
            cnt = cnt + beats.astype(jnp.int32)
        self_t = jnp.where(cnt < n_sel, 0.0, NEG)
        if n_slc < LANES:
            self_t = jnp.concatenate([self_t, jnp.zeros((LANES - n_slc, tq), F32)], axis=0)
        selfeat += [self_t.T.astype(BF16)] * NSA_GROUP
    qa = jnp.concatenate([qs, jnp.concatenate(selfeat, axis=0)], axis=1)

    tk = min(SLC_TK, t_len)

    hrows = rows // 2
    qa_h = (qa[:hrows], qa[hrows:])
    qpos_h = qpos_r[:hrows]

    def slc_tile(kt, carry, causal):
        ks = pl.ds(pl.multiple_of(kt * tk, tk), tk)
        out = []
        for hp in range(2):
            m_i, acc = carry[hp]
            s2 = _dot_nt(qa_h[hp], ka_ref[ks, :])
            if causal:
                kpos = kt * tk + lax.broadcasted_iota(jnp.int32, (hrows, tk), 1)
                s2 = jnp.where(kpos <= qpos_h, s2, NEG)
            m_n = jnp.maximum(m_i, jnp.max(s2, axis=-1, keepdims=True))
            al = jnp.exp(m_i - m_n)
            p2 = jnp.exp((s2 - m_n).astype(BF16))
            out.append((m_n, al * acc + _dot(p2, va_ref[hp, ks, :])))
        return tuple(out)

    lane_h = lax.broadcasted_iota(jnp.int32, (hrows, LANES), 1)

    def normalise(acc, hp):
        head = (lane_h >= NSA_DH) if hp == 1 else (lane_h < NSA_DH)
        return acc / jnp.where(head, pltpu.roll(acc, NSA_DH, 1), 1.0)

    n_full = (i * tq) // tk
    init = (jnp.full((hrows, 1), NEG, F32), jnp.zeros((hrows, LANES), F32))
    carry = lax.fori_loop(0, n_full, functools.partial(slc_tile, causal=False), (init, init))
    carry = slc_tile(n_full, carry, True)
    o_s = jnp.concatenate([normalise(carry[0][1], 0), normalise(carry[1][1], 1)], axis=0)

    lw = WINDOW + tq
    ws = pl.ds(pl.multiple_of(i * tq, tq), lw)
    qw = jnp.concatenate([qs, jnp.where(lax.broadcasted_iota(jnp.int32, (rows, LANES), 1) == 0, 1.0, 0.0
                                        ).astype(BF16)], axis=1)
    cr = (lax.broadcasted_iota(jnp.int32, (hrows, tq), 1)
          - lax.broadcasted_iota(jnp.int32, (hrows, tq), 0) % tq)
    o_w = []
    for hp in range(2):
        s3 = _dot_nt(qw[hp * hrows:(hp + 1) * hrows], wka_ref[ws, :])
        s3 = jnp.concatenate([jnp.where(cr > 0, s3[:, :tq], NEG), s3[:, tq:lw - tq],
                              jnp.where(cr <= 0, s3[:, lw - tq:], NEG)], axis=1)
        m3 = jnp.max(s3, axis=-1, keepdims=True)
        p3 = jnp.exp((s3 - m3).astype(BF16))
        o_w.append(normalise(_dot(p3, wva_ref[hp, ws, :]), hp))
    o_w = jnp.concatenate(o_w, axis=0)

    sig_hi = sig_gates.astype(BF16)
    sig_lo = (sig_gates - sig_hi.astype(F32)).astype(BF16)
    out = None
    for br, ob in enumerate((o_c, o_s, o_w)):
        parts = []
        for hp in range(2):
            for gp in range(2):
                r0 = (hp * NSA_GROUP + 2 * gp) * tq
                parts.append(_place_pair(ob[r0:r0 + tq], ob[r0 + tq:r0 + 2 * tq], hp, lane))
        gexp = _dot(sig_hi, gex_ref[br]) + _dot(sig_lo, gex_ref[br])
        term = gexp * jnp.concatenate(parts, axis=1)
        out = term if out is None else out + term
    o_ref[...] = (out * _silu(gate_ref[...])).astype(o_ref.dtype)


def _nsa_prompt(q, kvb, kcv, g, b, t):
    assert t % TQ == 0 and t % SLC_L == 0 and (t <= SLC_TK or t % SLC_TK == 0)
    n_sub = t // CMP_D
    n_slc = t // SLC_L
    assert n_slc <= LANES
    q3 = q.reshape(b, t, D_NSA)
    kvb3 = kvb.reshape(b, t, 6 * KV_DIM)
    g3 = g.reshape(b, t, G_WIDTH)
    ovt = jnp.asarray(_overlap_t(n_slc, n_sub), BF16)
    gex = jnp.asarray(np.stack([_gate_expand(0), _gate_expand(1)]), BF16)
    pw = 2 * NSA_GROUP * NSA_DH

    def kvcol(which):
        return pl.BlockSpec((None, t, LANES), lambda bi, p, i: (bi, 0, 2 * which + p))

    return pl.pallas_call(
        _nsa_prompt_kernel,
        grid=(b, 2, t // TQ),
        in_specs=[pl.BlockSpec((None, TQ, pw), lambda bi, p, i: (bi, i, p)),
                  kvcol(2), kvcol(3), kvcol(4), kvcol(5),
                  pl.BlockSpec((None, None, n_sub, LANES), lambda bi, p, i: (bi, 0, 0, p)),
                  pl.BlockSpec((None, None, n_sub, LANES), lambda bi, p, i: (bi, 1, 0, p)),
                  pl.BlockSpec((None, TQ, pw), lambda bi, p, i: (bi, i, p)),
                  pl.BlockSpec((None, TQ, LANES), lambda bi, p, i: (bi, i, D_NSA // LANES)),
                  pl.BlockSpec(ovt.shape, lambda bi, p, i: (0, 0)),
                  pl.BlockSpec((None, 3, LANES, pw), lambda bi, p, i: (p, 0, 0, 0))],
        out_specs=pl.BlockSpec((None, TQ, pw), lambda bi, p, i: (bi, i, p)),
        out_shape=jax.ShapeDtypeStruct((b, t, D_NSA), BF16),
        scratch_shapes=[pltpu.VMEM((t, 2 * LANES), BF16), pltpu.VMEM((2, t, LANES), BF16),
                        pltpu.VMEM((t + WINDOW, 2 * LANES), BF16), pltpu.VMEM((2, t + WINDOW, LANES), BF16)],
        compiler_params=_cparams(("arbitrary", "arbitrary", "arbitrary")),
        name="nsa_prompt",
    )(q3, kvb3, kvb3, kvb3, kvb3, kcv, kcv, g3, g3, ovt, gex)


CMP_PAGES = 32


def _cmp_bd_weights(cmp_w1, cmp_w2):
    w1 = cmp_w1.reshape(2, CMP_R, CMP_D, NSA_DH, CMP_HID)
    w1 = jnp.transpose(w1, (0, 2, 3, 1, 4)).reshape(2, CMP_D, NSA_DH, CMP_R * CMP_HID)
    eye2 = jnp.eye(2, dtype=w1.dtype)
    bd = jnp.einsum('wjdn,hg->wjhdgn', w1, eye2).reshape(2, CMP_D, LANES, 2 * CMP_R * CMP_HID)
    eye4 = jnp.eye(NSA_KV_HEADS, dtype=w1.dtype)
    w2q = jnp.einsum('wed,hg->whegd', cmp_w2, eye4).reshape(2, NSA_KV_HEADS, CMP_HID, KV_DIM)
    return bd.astype(BF16), w2q.astype(BF16)


def _cmp_sample_kernel(pt_ref, cache_ref, bd_ref, w2q_ref, pe_ref, w1f_ref, o_ref, tbuf_ref, buf_ref, acc_ref,
                       sem_ref, *, pages, n_b, n_c):
    w = pl.program_id(0)
    b = pl.program_id(1)
    c = pl.program_id(2)
    step = (w * n_b + b) * n_c + c
    total = 2 * n_b * n_c
    slot = step % 2
    rows = pages * PAGE_SIZE // CMP_D

    def copies(w_, b_, c_, slot_):
        return [pltpu.make_async_copy(
            cache_ref.at[pt_ref[b_, c_ * pages + p], pl.ds(pl.multiple_of(w_ * KV_DIM, KV_DIM), KV_DIM), :],
            tbuf_ref.at[slot_, p], sem_ref.at[slot_]) for p in range(pages)]

    @pl.when(step == 0)
    def _():
        for cp in copies(w, b, c, slot):
            cp.start()

    @pl.when(step + 1 < total)
    def _():
        nxt = step + 1
        c_n = nxt % n_c
        b_n = (nxt // n_c) % n_b
        w_n = nxt // (n_c * n_b)
        for cp in copies(w_n, b_n, c_n, 1 - slot):
            cp.start()

    for cp in copies(w, b, c, slot):
        cp.wait()

    def to_rows(p, carry):
        for half in range(2):
            buf_ref[half, pl.ds(pl.multiple_of(p * PAGE_SIZE, PAGE_SIZE), PAGE_SIZE), :] = (
                tbuf_ref[slot, p, half * LANES:(half + 1) * LANES, :].T)
        return carry

    lax.fori_loop(0, pages, to_rows, 0)

    pw = 2 * CMP_R * CMP_HID
    for half in range(2):
        acc = jnp.zeros((rows, pw), F32)
        for j in range(CMP_D):
            xj = buf_ref.at[half][pl.ds(j, rows, stride=CMP_D), :].astype(BF16)
            acc = acc + _dot(xj, bd_ref[j])
        acc_ref[pl.ds(pl.multiple_of(c * rows, rows), rows), half * pw:(half + 1) * pw] = acc

    @pl.when(c == n_c - 1)
    def _():
        n_sub = acc_ref.shape[0]
        pe_t = _pe_term(pe_ref, w1f_ref)
        out = jnp.zeros((n_sub, KV_DIM), F32)
        for h in range(NSA_KV_HEADS):
            c0 = h * CMP_R * CMP_HID
            nxt = pltpu.roll(acc_ref[:, c0 + CMP_HID:c0 + 2 * CMP_HID], n_sub - 1, 0)
            hid = pe_t + acc_ref[:, c0:c0 + CMP_HID] + nxt
            out = out + _dot(_silu(hid).astype(BF16), w2q_ref[h])
        o_ref[...] = out.astype(o_ref.dtype)


def _cmp_sample(cache_t, page_table, bd, w2q, cmp_w1, cmp_pe):
    n_b, n_pages = page_table.shape
    pages = min(CMP_PAGES, n_pages)
    assert n_pages % pages == 0
    n_c = n_pages // pages
    n_sub = n_pages * PAGE_SIZE // CMP_D
    pe = cmp_pe.reshape(2, 1, CMP_L * NSA_DH)
    w1f = cmp_w1.reshape(2, CMP_L * NSA_DH, CMP_HID)
    nbd = NSA_KV_HEADS * CMP_R * CMP_HID
    kern = functools.partial(_cmp_sample_kernel, pages=pages, n_b=n_b, n_c=n_c)
    return pl.pallas_call(
        kern,
        grid_spec=pltpu.PrefetchScalarGridSpec(
            num_scalar_prefetch=1,
            grid=(2, n_b, n_c),
            in_specs=[pl.BlockSpec(memory_space=pl.ANY),
                      pl.BlockSpec((None, CMP_D, LANES, nbd // 2), lambda w, b, c, pt: (w, 0, 0, 0)),
                      pl.BlockSpec((None, NSA_KV_HEADS, CMP_HID, KV_DIM), lambda w, b, c, pt: (w, 0, 0, 0)),
                      pl.BlockSpec((None, 1, CMP_L * NSA_DH), lambda w, b, c, pt: (w, 0, 0)),
                      pl.BlockSpec((None, CMP_L * NSA_DH, CMP_HID), lambda w, b, c, pt: (w, 0, 0))],
            out_specs=pl.BlockSpec((None, None, n_sub, KV_DIM), lambda w, b, c, pt: (b, w, 0, 0)),
            scratch_shapes=[pltpu.VMEM((2, pages, KV_DIM, PAGE_SIZE), F32),
                            pltpu.VMEM((2, pages * PAGE_SIZE, LANES), F32),
                            pltpu.VMEM((n_sub, nbd), F32),
                            pltpu.SemaphoreType.DMA((2,))]),
        out_shape=jax.ShapeDtypeStruct((n_b, 2, n_sub, KV_DIM), BF16),
        compiler_params=_cparams(("arbitrary", "arbitrary", "arbitrary")),
        name="cmp_sample",
    )(page_table, cache_t, bd, w2q, pe, w1f)


def _nsa_sample_a_kernel(qh_ref, kc_ref, vc_ref, ov_ref, oc_ref, sel_ref, *, past_len, n_slc):
    n_sub = kc_ref.shape[0]
    n_slc_pad = ov_ref.shape[1]
    n_sel = min(N_SEL, n_slc)
    qpos = past_len
    rowi = lax.broadcasted_iota(jnp.int32, (SUBLANES, n_sub), 0)
    rowq = lax.broadcasted_iota(jnp.int32, (SROWS, n_sub), 0)
    ncol = lax.broadcasted_iota(jnp.int32, (SROWS, n_sub), 1)
    msk = (ncol * CMP_D + (CMP_L - 1) <= qpos) & (ncol < n_sub - 1)
    psum = jnp.zeros((SUBLANES, n_sub), F32)
    for h in range(NSA_KV_HEADS):
        pr = h // 2
        qh = (qh_ref[h] * NSA_DH ** -0.5).astype(BF16)
        s = _dot_nt(qh, kc_ref[:, pr * LANES:(pr + 1) * LANES])
        s = jnp.where(msk, s, NEG)
        m = jnp.max(s, axis=-1, keepdims=True)
        p = jnp.where(msk, jnp.exp(s - m), 0.0)
        den = jnp.sum(p, axis=-1, keepdims=True)
        p = p / jnp.where(den > 0, den, 1.0)
        oc_ref[h] = _dot(p.astype(BF16), vc_ref[:, pr * LANES:(pr + 1) * LANES])
        ph = jnp.sum(jnp.where(rowq < NSA_GROUP, p, 0.0), axis=0, keepdims=True)
        psum = jnp.where(rowi == h, ph, psum)

    hi, mid, lo = _split3(psum)
    imp = _dot(hi, ov_ref[...]) + _dot(mid, ov_ref[...]) + _dot(lo, ov_ref[...])
    jb = lax.broadcasted_iota(jnp.int32, (SUBLANES, n_slc_pad), 1)
    cur = qpos // SLC_L
    forced = (jb == 0) | (jb == cur) | (jb == cur - 1)
    valid = jb * SLC_L <= qpos
    score = jnp.where(forced, FORCED_SCORE, jnp.where(valid, imp, -1.0))
    score = jnp.where(jb < n_slc, score, -2.0)
    cnt = jnp.zeros((SUBLANES, n_slc_pad), jnp.int32)
    for jp in range(n_slc):
        r = score[:, jp:jp + 1]
        beats = (r > score) | ((r == score) & (jb > jp))
        cnt = cnt + beats.astype(jnp.int32)
    cnt = jnp.where(jb < n_slc, cnt, n_slc_pad)
    lane = lax.broadcasted_iota(jnp.int32, (SUBLANES, LANES), 1)
    idx = jnp.zeros((SUBLANES, LANES), jnp.int32)
    for slot in range(n_sel):
        pick = jnp.sum(jnp.where(cnt == slot, jb, 0), axis=1, keepdims=True)
        idx = jnp.where(lane == slot, pick, idx)
    sel_ref[...] = idx


def _nsa_sample_a(qh, kcv, past_len):
    n_b = qh.shape[0]
    n_sub = kcv.shape[2]
    n_slc = -(-(past_len + 1) // SLC_L)
    n_slc_pad = -(-n_slc // LANES) * LANES
    ov = np.zeros((n_sub, n_slc_pad), np.float32)
    ov[:, :n_slc] = _overlap_t(n_slc, n_sub).T
    ov = jnp.asarray(ov, BF16)
    kern = functools.partial(_nsa_sample_a_kernel, past_len=past_len, n_slc=n_slc)
    return pl.pallas_call(
        kern,
        grid=(n_b,),
        in_specs=[pl.BlockSpec((None, NSA_KV_HEADS, SROWS, LANES), lambda b: (b, 0, 0, 0)),
                  pl.BlockSpec((None, None, n_sub, KV_DIM), lambda b: (b, 0, 0, 0)),
                  pl.BlockSpec((None, None, n_sub, KV_DIM), lambda b: (b, 1, 0, 0)),
                  pl.BlockSpec(ov.shape, lambda b: (0, 0))],
        out_specs=[pl.BlockSpec((None, NSA_KV_HEADS, SROWS, LANES), lambda b: (b, 0, 0, 0)),
                   pl.BlockSpec((None, SUBLANES, LANES), lambda b: (b, 0, 0))],
        out_shape=[jax.ShapeDtypeStruct((n_b, NSA_KV_HEADS, SROWS, LANES), F32),
                   jax.ShapeDtypeStruct((n_b, SUBLANES, LANES), jnp.int32)],
        compiler_params=_cparams(("arbitrary",)),
        name="nsa_sample_a",
    )(qh, kcv, kcv, ov), n_slc


def _nsa_sample_b_kernel(pt_ref, sel_ref, qh_ref, cache_ref, new_ref, win_ref, oc_ref, gates_ref, gate_ref,
                         o_ref, kbuf_ref, vbuf_ref, sem_ref, *, past_len, n_slc, n_sel, n_b):
    b = pl.program_id(0)
    slot = b % 2
    n_blocks_cached = past_len // SLC_L
    per_page = PAGE_SIZE // SLC_L

    def copies(b_, slot_):
        out = []
        for h in range(NSA_KV_HEADS):
            for k in range(n_sel):
                j = jnp.minimum(sel_ref[b_, h, k], n_blocks_cached - 1)
                page = pt_ref[b_, j // per_page]
                for which, buf in ((2, kbuf_ref), (3, vbuf_ref)):
                    out.append(pltpu.make_async_copy(
                        cache_ref.at[page, pl.ds(which * KV_DIM + (h // 2) * LANES, LANES), :],
                        buf.at[slot_, h, :, pl.ds(k * PAGE_SIZE, PAGE_SIZE)],
                        sem_ref.at[slot_]))
        return out

    @pl.when(b == 0)
    def _():
        for cp in copies(b, slot):
            cp.start()

    @pl.when(b + 1 < n_b)
    def _():
        for cp in copies(b + 1, 1 - slot):
            cp.start()

    for cp in copies(b, slot):
        cp.wait()

    qpos = past_len
    nk = n_sel * PAGE_SIZE
    lane_k = lax.broadcasted_iota(jnp.int32, (1, nk), 1)
    n_buf = win_ref.shape[1]
    widx = lax.broadcasted_iota(jnp.int32, (1, n_buf), 1)
    rel = n_buf - widx
    wmask = (rel >= 0) & (rel < WINDOW) & (past_len - n_buf + widx >= 0)
    rowg = lax.broadcasted_iota(jnp.int32, (SROWS, LANES), 0)

    for h in range(NSA_KV_HEADS):
        pr = h // 2
        ls = slice(pr * LANES, (pr + 1) * LANES)
        qf = qh_ref[h] * NSA_DH ** -0.5
        qh = qf.astype(BF16)

        def new_key(row):
            return new_ref[row:row + 1, ls].astype(BF16).astype(F32)

        blk = jnp.zeros((1, nk), jnp.int32)
        n_cur = jnp.int32(0)
        for k in range(n_sel):
            sj = sel_ref[b, h, k]
            blk = blk + jnp.where(lane_k // PAGE_SIZE == k, sj, 0)
            n_cur = n_cur + (sj == qpos // SLC_L).astype(jnp.int32)
        has_new = (jnp.zeros((1, 1), jnp.int32) + n_cur) > 0
        kvalid = ((lane_k % PAGE_SIZE) // SLC_L == blk % per_page) & (blk < n_blocks_cached)
        s = _dot(qh, kbuf_ref[slot, h].astype(BF16))
        s = jnp.where(kvalid, s, NEG)
        s_new = jnp.sum(qf * new_key(2), axis=-1, keepdims=True)
        s_new = jnp.where(has_new, s_new, NEG)
        m = jnp.maximum(jnp.max(s, axis=-1, keepdims=True), s_new)
        p = jnp.where(kvalid, jnp.exp(s - m), 0.0)
        p_new = jnp.where(has_new, jnp.exp(s_new - m), 0.0)
        den = jnp.sum(p, axis=-1, keepdims=True) + p_new
        den = jnp.where(den > 0, den, 1.0)
        o_s = (_dot_nt(p.astype(BF16), vbuf_ref[slot, h].astype(BF16)) + p_new * new_key(3)) / den

        s = _dot(qh, win_ref[pr * LANES:(pr + 1) * LANES, :].astype(BF16))
        s = jnp.where(wmask, s, NEG)
        s_new = jnp.sum(qf * new_key(4), axis=-1, keepdims=True)
        m = jnp.maximum(jnp.max(s, axis=-1, keepdims=True), s_new)
        p = jnp.where(wmask, jnp.exp(s - m), 0.0)
        p_new = jnp.exp(s_new - m)
        den = jnp.sum(p, axis=-1, keepdims=True) + p_new
        wv = win_ref[KV_DIM + pr * LANES:KV_DIM + (pr + 1) * LANES, :].astype(BF16)
        o_w = (_dot_nt(p.astype(BF16), wv) + p_new * new_key(5)) / den

        g = jax.nn.sigmoid(gates_ref[:, h])
        o = g[0] * oc_ref[h] + g[1] * o_s + g[2] * o_w
        o_ref[h] = jnp.where(rowg < NSA_GROUP, o * _silu(gate_ref[h]), 0.0)


def _nsa_sample_b(page_table, sel, qh, cache_t, kv_new, win_t, o_c, gates_l, gate_l, past_len, n_slc):
    n_b = qh.shape[0]
    n_sel = sel.shape[2]
    n_buf = win_t.shape[2]
    kern = functools.partial(_nsa_sample_b_kernel, past_len=past_len, n_slc=n_slc, n_sel=n_sel, n_b=n_b)
    hb = (None, NSA_KV_HEADS, SROWS, LANES)
    return pl.pallas_call(
        kern,
        grid_spec=pltpu.PrefetchScalarGridSpec(
            num_scalar_prefetch=2,
            grid=(n_b,),
            in_specs=[pl.BlockSpec(hb, lambda b, pt, sl: (b, 0, 0, 0)),
                      pl.BlockSpec(memory_space=pl.ANY),
                      pl.BlockSpec((None, 6, KV_DIM), lambda b, pt, sl: (b, 0, 0)),
                      pl.BlockSpec((None, 2 * KV_DIM, n_buf), lambda b, pt, sl: (b, 0, 0)),
                      pl.BlockSpec(hb, lambda b, pt, sl: (b, 0, 0, 0)),
                      pl.BlockSpec((None, 3, NSA_KV_HEADS, SROWS, LANES), lambda b, pt, sl: (b, 0, 0, 0, 0)),
                      pl.BlockSpec(hb, lambda b, pt, sl: (b, 0, 0, 0))],
            out_specs=pl.BlockSpec(hb, lambda b, pt, sl: (b, 0, 0, 0)),
            scratch_shapes=[pltpu.VMEM((2, NSA_KV_HEADS, LANES, n_sel * PAGE_SIZE), F32),
                            pltpu.VMEM((2, NSA_KV_HEADS, LANES, n_sel * PAGE_SIZE), F32),
                            pltpu.SemaphoreType.DMA((2,))]),
        out_shape=jax.ShapeDtypeStruct((n_b, NSA_KV_HEADS, SROWS, LANES), F32),
        compiler_params=_cparams(("arbitrary",)),
        name="nsa_sample_b",
    )(page_table, sel, qh, cache_t, kv_new, win_t, o_c, gates_l, gate_l)


def _head_layout(x):
    n_b = x.shape[0]
    x = x.astype(F32)
    z = jnp.zeros_like(x)
    even = jnp.concatenate([x, z], axis=-1)
    odd = jnp.concatenate([z, x], axis=-1)
    par = (jnp.arange(NSA_KV_HEADS) % 2).reshape(1, NSA_KV_HEADS, 1, 1)
    y = jnp.where(par == 0, even, odd)
    return jnp.concatenate([y, jnp.zeros((n_b, NSA_KV_HEADS, SROWS - NSA_GROUP, LANES), F32)], axis=2)


def _sample_path(x_sample, cache_kv, cache_win, state, page_table, w_p, w_out_b, hg_lb, norm_g,
                 cmp_w1, cmp_w2, cmp_pe, ln_g, ln_b):
    n_b, t_s, _ = x_sample.shape
    assert t_s == 1, "single-token decode"
    n_pool = cache_kv.shape[0]
    n_pages = page_table.shape[1]
    past_len = n_pages * PAGE_SIZE
    xs = x_sample.reshape(n_b, D_MODEL)
    hg, q, kv, _, g = _inproj(xs, w_p, n_b)
    o_hg, s_new = _hgrn_sample(hg, state, hg_lb, norm_g)

    cache_t = jnp.transpose(cache_kv, (0, 2, 3, 4, 1)).reshape(n_pool, 4 * KV_DIM, PAGE_SIZE)
    bd, w2q = _cmp_bd_weights(cmp_w1, cmp_w2)
    kcv = _cmp_sample(cache_t, page_table, bd, w2q, cmp_w1, cmp_pe)

    qh = _head_layout(q.reshape(n_b, NSA_KV_HEADS, NSA_GROUP, NSA_DH))
    (o_c, sel), n_slc = _nsa_sample_a(qh, kcv, past_len)
    n_sel = min(N_SEL, n_slc)
    sel = sel[:, :NSA_KV_HEADS, :n_sel]

    gates = g[:, D_NSA:D_NSA + 3 * NSA_HEADS].reshape(n_b, 3, NSA_KV_HEADS, NSA_GROUP, 1)
    gates_l = _head_layout_b(jnp.broadcast_to(gates, (n_b, 3, NSA_KV_HEADS, NSA_GROUP, NSA_DH)))
    gate_l = _head_layout(g[:, :D_NSA].reshape(n_b, NSA_KV_HEADS, NSA_GROUP, NSA_DH))
    n_buf = cache_win.shape[1]
    win_t = jnp.transpose(cache_win, (0, 2, 3, 4, 1)).reshape(n_b, 2 * KV_DIM, n_buf)
    o = _nsa_sample_b(page_table, sel, qh, cache_t, kv.reshape(n_b, 6, KV_DIM), win_t, o_c, gates_l, gate_l,
                      past_len, n_slc)
    o = o.reshape(n_b, NSA_KV_HEADS, SROWS, 2, NSA_DH)[:, :, :NSA_GROUP].sum(axis=3)
    o_nsa = o.reshape(n_b, D_NSA).astype(BF16)

    y = _outproj(xs, o_hg.reshape(n_b, D_HG).astype(BF16), o_nsa, w_out_b, ln_g, ln_b, n_b)
    kv6 = kv.reshape(n_b, 1, 6, NSA_KV_HEADS, NSA_DH)
    win_cat = jnp.concatenate([cache_win, kv6[:, :, 4:].astype(cache_win.dtype)], axis=1)
    return (y.reshape(n_b, 1, D_MODEL), kv6[None, :, :, :4].astype(cache_kv.dtype), win_cat[None, :, 1:],
            s_new[None].astype(state.dtype))


def _head_layout_b(x):
    n_b = x.shape[0]
    y = _head_layout(x.reshape(n_b * 3, NSA_KV_HEADS, NSA_GROUP, NSA_DH))
    return y.reshape(n_b, 3, NSA_KV_HEADS, SROWS, LANES)


def kernel(x_prompt, x_sample, cache_kv, cache_win, state_hgrn, page_table, w_in, hg_lb, hg_norm_g,
           cmp_w1, cmp_w2, cmp_pe, w_out, ln_g, ln_b):
    assert w_in.shape[0] == 1, "single-layer decoder"
    b, t, _ = x_prompt.shape
    w_p = _prep_w_in(w_in[0])
    w_out_b = w_out[0].astype(BF16)
    w1p, w2p = _cmp_weights(cmp_w1[0], cmp_w2[0])

    xp = x_prompt.reshape(b * t, D_MODEL)
    tm = 1024 if (b * t) % 1024 == 0 else 256
    hg, q, kv, kvb, g = _inproj(xp, w_p, tm)
    o_hg, s_p = _hgrn_prompt(hg, hg_lb, hg_norm_g[0], b, t)
    kcv = _cmp_prompt(kv, w1p, w2p, cmp_w1[0], cmp_pe[0], b, t)
    o_nsa = _nsa_prompt(q, kvb, kcv, g, b, t)
    y_p = _outproj(xp, o_hg.reshape(b * t, D_HG), o_nsa.reshape(b * t, D_NSA), w_out_b,
                   ln_g[0], ln_b[0], 512 if (b * t) % 512 == 0 else 256)
    kv6 = kv.reshape(b, t, 6, NSA_KV_HEADS, NSA_DH)
    wlen = min(WINDOW, t)

    y_s, kv_s, win_s, s_s = _sample_path(x_sample, cache_kv[0], cache_win[0], state_hgrn[0], page_table, w_p,
                                         w_out_b, hg_lb, hg_norm_g[0], cmp_w1[0], cmp_w2[0], cmp_pe[0],
                                         ln_g[0], ln_b[0])
    return (y_p.reshape(b, t, D_MODEL), y_s, kv6[None, :, :, :4], kv6[None, :, t - wlen:, 4:],
            s_p[None].astype(x_prompt.dtype), kv_s, win_s, s_s)
```

```python
import functools

import numpy as np
import jax
import jax.numpy as jnp
from jax import lax
from jax.experimental import pallas as pl
from jax.experimental.pallas import tpu as pltpu

F32 = jnp.float32
BF16 = jnp.bfloat16

D_MODEL = 2048
D_HG = 1024
D_NSA = 1024
HG_DK = 128
HG_DV = 128
HG_HEADS = 8
NSA_DH = 64
NSA_HEADS = 16
NSA_KV_HEADS = 4
NSA_GROUP = 4
KV_DIM = 256
CMP_L = 32
CMP_D = 16
CMP_R = 2
CMP_HID = 128
SLC_L = 64
N_SEL = 16
WINDOW = 512
PAGE_SIZE = 128
ALPHA = 2.0 ** 0.25
LN_EPS = 1e-5
RMS_EPS = 1e-6
FORCED_SCORE = 1e4
NEG = -1e30

LANES = 128
SUBLANES = 8
VMEM_LIMIT = 56 * 1024 * 1024
SROWS = 16

COL_TILE = 512
N_HG_TILES = 4 * D_HG // COL_TILE
N_Q_TILES = D_NSA // COL_TILE
N_KV_TILES = 6 * KV_DIM // COL_TILE
G_WIDTH = 1536
N_G_TILES = G_WIDTH // COL_TILE
N_COL_TILES = N_HG_TILES + N_Q_TILES + N_KV_TILES + N_G_TILES
D_IN_PAD = N_COL_TILES * COL_TILE

HG_CHUNK = 128
HG_LEVELS = 4
HG_HPS = 2


def _cparams(sem, flags=None):
    return pltpu.CompilerParams(dimension_semantics=sem, vmem_limit_bytes=VMEM_LIMIT, flags=flags)


def _dot(a, b):
    return jnp.dot(a, b, preferred_element_type=F32)


def _dot_nt(a, b):
    return lax.dot_general(a, b, (((1,), (1,)), ((), ())), preferred_element_type=F32)


def _split3(x):
    hi = x.astype(BF16)
    r1 = x - hi.astype(F32)
    mid = r1.astype(BF16)
    lo = (r1 - mid.astype(F32)).astype(BF16)
    return hi, mid, lo


def _dot3(a_bf, x):
    hi, mid, lo = _split3(x)
    return _dot(a_bf, hi) + _dot(a_bf, mid) + _dot(a_bf, lo)


def _dot3_nt(a_bf, x):
    hi, mid, lo = _split3(x)
    return _dot_nt(a_bf, hi) + _dot_nt(a_bf, mid) + _dot_nt(a_bf, lo)


def _silu(x):
    return x * jax.nn.sigmoid(x)


def _inproj_kernel(x_ref, w_ref, hg_ref, q_ref, kv_ref, kvb_ref, g_ref, xb_ref):
    j = pl.program_id(1)

    @pl.when(j == 0)
    def _():
        xb_ref[...] = x_ref[...].astype(BF16)

    acc = _dot(xb_ref[...], w_ref[...])
    q0 = N_HG_TILES
    kv0 = q0 + N_Q_TILES
    g0 = kv0 + N_KV_TILES

    @pl.when(j < q0)
    def _():
        hg_ref[...] = acc

    @pl.when((j >= q0) & (j < kv0))
    def _():
        q_ref[...] = acc.astype(BF16)

    @pl.when((j >= kv0) & (j < g0))
    def _():
        kv_ref[...] = acc
        kvb_ref[...] = acc.astype(BF16)

    @pl.when(j >= g0)
    def _():
        g_ref[...] = acc


def _inproj(x2d, w_p, tm):
    m = x2d.shape[0]
    assert m % tm == 0
    q0 = N_HG_TILES
    kv0 = q0 + N_Q_TILES
    g0 = kv0 + N_KV_TILES

    def clampmap(lo, n):
        return lambda i, j: (i, jnp.clip(j - lo, 0, n - 1))

    return pl.pallas_call(
        _inproj_kernel,
        grid=(m // tm, N_COL_TILES),
        in_specs=[pl.BlockSpec((tm, D_MODEL), lambda i, j: (i, 0)),
                  pl.BlockSpec((D_MODEL, COL_TILE), lambda i, j: (0, j))],
        out_specs=[pl.BlockSpec((tm, COL_TILE), clampmap(0, N_HG_TILES)),
                   pl.BlockSpec((tm, COL_TILE), clampmap(q0, N_Q_TILES)),
                   pl.BlockSpec((tm, COL_TILE), clampmap(kv0, N_KV_TILES)),
                   pl.BlockSpec((tm, COL_TILE), clampmap(kv0, N_KV_TILES)),
                   pl.BlockSpec((tm, COL_TILE), clampmap(g0, N_G_TILES))],
        out_shape=[jax.ShapeDtypeStruct((m, 4 * D_HG), F32),
                   jax.ShapeDtypeStruct((m, D_NSA), BF16),
                   jax.ShapeDtypeStruct((m, 6 * KV_DIM), F32),
                   jax.ShapeDtypeStruct((m, 6 * KV_DIM), BF16),
                   jax.ShapeDtypeStruct((m, G_WIDTH), F32)],
        scratch_shapes=[pltpu.VMEM((tm, D_MODEL), BF16)],
        compiler_params=_cparams(("arbitrary", "arbitrary")),
        name="inproj",
    )(x2d, w_p)


def _prep_w_in(w_in):
    o_nq = 4 * D_HG
    o_kv = o_nq + D_NSA
    o_gates = o_kv + 6 * KV_DIM
    o_gate = o_gates + 3 * NSA_HEADS
    pad = G_WIDTH - D_NSA - 3 * NSA_HEADS
    w = jnp.concatenate([w_in[:, :o_gates], w_in[:, o_gate:], w_in[:, o_gates:o_gate],
                         jnp.zeros((D_MODEL, pad), w_in.dtype)], axis=1)
    return w.astype(BF16)


def _outproj_kernel(x_ref, ohg_ref, onsa_ref, w_ref, g_ref, b_ref, y_ref):
    y = _dot(ohg_ref[...], w_ref[0:D_HG, :]) + _dot(onsa_ref[...], w_ref[D_HG:D_MODEL, :])
    z = ALPHA * x_ref[...] + y
    mu = jnp.mean(z, axis=-1, keepdims=True)
    zc = z - mu
    var = jnp.mean(zc * zc, axis=-1, keepdims=True)
    y_ref[...] = zc * lax.rsqrt(var + LN_EPS) * g_ref[...] + b_ref[...]


def _outproj(x2d, o_hg, o_nsa, w_out_b, ln_g, ln_b, tm):
    m = x2d.shape[0]
    assert m % tm == 0
    return pl.pallas_call(
        _outproj_kernel,
        grid=(m // tm,),
        in_specs=[pl.BlockSpec((tm, D_MODEL), lambda i: (i, 0)),
                  pl.BlockSpec((tm, D_HG), lambda i: (i, 0)),
                  pl.BlockSpec((tm, D_NSA), lambda i: (i, 0)),
                  pl.BlockSpec((D_MODEL, D_MODEL), lambda i: (0, 0)),
                  pl.BlockSpec((1, D_MODEL), lambda i: (0, 0)),
                  pl.BlockSpec((1, D_MODEL), lambda i: (0, 0))],
        out_specs=pl.BlockSpec((tm, D_MODEL), lambda i: (i, 0)),
        out_shape=jax.ShapeDtypeStruct((m, D_MODEL), F32),
        compiler_params=_cparams(("arbitrary",)),
        name="outproj_ln",
    )(x2d, o_hg, o_nsa, w_out_b, ln_g.reshape(1, D_MODEL), ln_b.reshape(1, D_MODEL))


def _hgrn_gmat():
    c = HG_CHUNK
    tri = np.tril(np.ones((c, c), np.float32))
    mats = [tri]
    for lvl in range(1, HG_LEVELS + 1):
        half = SUBLANES * 2 ** (lvl - 1)
        t = np.arange(c)
        mid = (t // (2 * half)) * (2 * half) + half
        mats.append(tri - tri[mid - 1])
    return jnp.asarray(np.concatenate(mats, axis=0), BF16)


def _hgrn_lower_bound(lb_ref):
    raw = lb_ref[...]
    mx = jnp.max(raw, axis=0, keepdims=True)
    e = jnp.exp(raw - mx)
    return e[0:1, :] / jnp.sum(e, axis=0, keepdims=True)


def _hgrn_kernel(lb_ref, ng_ref, gm_ref, q_ref, f_ref, i_ref, g_ref, o_ref, s_ref, st_ref):
    c = HG_CHUNK
    t_len = q_ref.shape[0]
    lb2 = _hgrn_lower_bound(lb_ref)
    st_ref[...] = jnp.zeros_like(st_ref)

    row = lax.broadcasted_iota(jnp.int32, (c, HG_DK), 0)
    sub3 = lax.broadcasted_iota(jnp.int32, (c // SUBLANES, SUBLANES, HG_DK), 1)
    r2 = lax.broadcasted_iota(jnp.int32, (c, c), 0)
    c2 = lax.broadcasted_iota(jnp.int32, (c, c), 1)

    n_chunks = t_len // c

    def gates(ci):
        fr2 = f_ref[pl.ds(pl.multiple_of(ci * c, c), c), :]
        logf2 = jnp.log(lb2 + (1.0 - lb2) * jax.nn.sigmoid(fr2))
        return (1.0 - lb2) * jax.nn.sigmoid(-fr2), _dot3(gm_ref[...], logf2)

    def chunk(ci, carry):
        k2, dall2 = carry
        nxt = gates(jnp.minimum(ci + 1, n_chunks - 1))
        sl = pl.ds(pl.multiple_of(ci * c, c), c)
        for hh in range(HG_HPS):
            hs = slice(hh * HG_DK, (hh + 1) * HG_DK)
            one_head(sl, hh, q_ref[sl, hs], k2[:, hs], i_ref[sl, hs], dall2[:, hs], g_ref[sl, hs],
                     ng_ref[:, hs])
        return nxt

    def one_head(sl, hh, q, k, v, dall, gate, ng):
        a = dall[0:c]

        q3 = q.reshape(c // SUBLANES, SUBLANES, HG_DK)
        k3 = k.reshape(c // SUBLANES, SUBLANES, HG_DK)
        v3 = v.reshape(c // SUBLANES, SUBLANES, HG_DV)
        a3 = a.reshape(c // SUBLANES, SUBLANES, HG_DK)
        od = jnp.sum(q3 * k3, axis=-1, keepdims=True) * v3
        for d in range(1, SUBLANES):
            ks = pltpu.roll(k3, d, 1)
            as_ = pltpu.roll(a3, d, 1)
            vs = pltpu.roll(v3, d, 1)
            ok = sub3 >= d
            w = jnp.where(ok, q3 * ks * jnp.exp(jnp.minimum(a3 - as_, 0.0)), 0.0)
            od = od + jnp.sum(w, axis=-1, keepdims=True) * vs
        od = od.reshape(c, HG_DV)

        att = jnp.zeros((c, c), F32)
        for lvl in range(1, HG_LEVELS + 1):
            half = SUBLANES * 2 ** (lvl - 1)
            dl = dall[lvl * c:(lvl + 1) * c]
            e = jnp.exp(-jnp.abs(dl))
            upper = (row % (2 * half)) >= half
            qt = jnp.where(upper, q * e, 0.0).astype(BF16)
            kt = jnp.where(upper, 0.0, k * e).astype(BF16)
            al = _dot_nt(qt, kt)
            if 2 * half < c:
                al = jnp.where((r2 // (2 * half)) == (c2 // (2 * half)), al, 0.0)
            att = att + al
        vb = v.astype(BF16)
        o = od + _dot(att.astype(BF16), vb)

        st = st_ref[hh]
        o = o + _dot_nt((q * jnp.exp(a)).astype(BF16), st.astype(BF16))
        alast = a[c - 1:c, :]
        kd = (k * jnp.exp(alast - a)).astype(BF16)
        st_ref[hh] = st * jnp.exp(alast) + _dot(v.T.astype(BF16), kd)

        o = o * lax.rsqrt(jnp.mean(o * o, axis=-1, keepdims=True) + RMS_EPS)
        o_ref[sl, hh * HG_DV:(hh + 1) * HG_DV] = (o * ng * _silu(gate)).astype(o_ref.dtype)

    lax.fori_loop(0, n_chunks, chunk, gates(0))
    for hh in range(HG_HPS):
        s_ref[hh] = st_ref[hh].T


def _hgrn_prompt(hg, hg_lb, norm_g, b, t):
    assert t % HG_CHUNK == 0
    hg3 = hg.reshape(b, t, 4 * D_HG)
    gm = _hgrn_gmat()

    n_hb = HG_HEADS // HG_HPS
    wide = HG_HPS * HG_DK

    def col(k):
        return pl.BlockSpec((None, t, wide), lambda bi, h: (bi, 0, k * n_hb + h))

    return pl.pallas_call(
        _hgrn_kernel,
        grid=(b, n_hb),
        in_specs=[pl.BlockSpec((2, wide), lambda bi, h: (0, h)),
                  pl.BlockSpec((1, wide), lambda bi, h: (0, h)),
                  pl.BlockSpec(gm.shape, lambda bi, h: (0, 0)),
                  col(0), col(1), col(2), col(3)],
        out_specs=[pl.BlockSpec((None, t, wide), lambda bi, h: (bi, 0, h)),
                   pl.BlockSpec((None, HG_HPS, HG_DK, HG_DV), lambda bi, h: (bi, h, 0, 0))],
        out_shape=[jax.ShapeDtypeStruct((b, t, D_HG), BF16),
                   jax.ShapeDtypeStruct((b, HG_HEADS, HG_DK, HG_DV), F32)],
        scratch_shapes=[pltpu.VMEM((HG_HPS, HG_DV, HG_DK), F32)],
        compiler_params=_cparams(("arbitrary", "arbitrary")),
        name="hgrn_prompt",
    )(hg_lb, norm_g.reshape(1, D_HG), gm, hg3, hg3, hg3, hg3)


def _hgrn_step_kernel(lb_ref, ng_ref, x_ref, s_ref, o_ref, so_ref):
    raw = lb_ref[...]
    mx = jnp.max(raw, axis=0)
    e0 = jnp.exp(raw[0] - mx)
    lb_all = e0 / (e0 + jnp.exp(raw[1] - mx))
    x = x_ref[...]
    eye = (lax.broadcasted_iota(jnp.int32, (HG_DK, HG_DK), 0)
           == lax.broadcasted_iota(jnp.int32, (HG_DK, HG_DK), 1))

    def column(rowvec):
        return jnp.sum(jnp.where(eye, rowvec, 0.0), axis=1, keepdims=True)

    for h in range(HG_HEADS):
        lb = lb_all[h:h + 1]
        q = x[h:h + 1]
        fr = x[HG_HEADS + h:HG_HEADS + h + 1]
        v = x[2 * HG_HEADS + h:2 * HG_HEADS + h + 1]
        gate = x[3 * HG_HEADS + h:3 * HG_HEADS + h + 1]
        f = lb + (1.0 - lb) * jax.nn.sigmoid(fr)
        k = (1.0 - lb) * jax.nn.sigmoid(-fr)
        s_new = s_ref[h] * column(f) + column(k) * v
        so_ref[h] = s_new
        o = jnp.sum(column(q) * s_new, axis=0, keepdims=True)
        o = o * lax.rsqrt(jnp.mean(o * o, axis=-1, keepdims=True) + RMS_EPS)
        o_ref[h:h + 1, :] = o * ng_ref[h:h + 1, :] * _silu(gate)


def _hgrn_sample(hg_s, state, hg_lb, norm_g):
    b = hg_s.shape[0]
    x3 = hg_s.reshape(b, 4 * HG_HEADS, HG_DK)
    return pl.pallas_call(
        _hgrn_step_kernel,
        grid=(b,),
        in_specs=[pl.BlockSpec((2, HG_HEADS, HG_DK), lambda i: (0, 0, 0)),
                  pl.BlockSpec((HG_HEADS, HG_DV), lambda i: (0, 0)),
                  pl.BlockSpec((None, 4 * HG_HEADS, HG_DK), lambda i: (i, 0, 0)),
                  pl.BlockSpec((None, HG_HEADS, HG_DK, HG_DV), lambda i: (i, 0, 0, 0))],
        out_specs=[pl.BlockSpec((None, HG_HEADS, HG_DV), lambda i: (i, 0, 0)),
                   pl.BlockSpec((None, HG_HEADS, HG_DK, HG_DV), lambda i: (i, 0, 0, 0))],
        out_shape=[jax.ShapeDtypeStruct((b, HG_HEADS, HG_DV), F32),
                   jax.ShapeDtypeStruct((b, HG_HEADS, HG_DK, HG_DV), F32)],
        compiler_params=_cparams(("arbitrary",)),
        name="hgrn_sample",
    )(hg_lb.reshape(2, HG_HEADS, HG_DK), norm_g.reshape(HG_HEADS, HG_DV), x3, state)


def _cmp_weights(cmp_w1, cmp_w2):
    w1 = cmp_w1.reshape(2, CMP_R, CMP_D, NSA_DH, CMP_HID)
    w1 = jnp.transpose(w1, (0, 2, 3, 1, 4)).reshape(2, CMP_D, NSA_DH, CMP_R * CMP_HID)
    z1 = jnp.zeros_like(w1)
    w1p = jnp.stack([jnp.concatenate([w1, z1], axis=2), jnp.concatenate([z1, w1], axis=2)], axis=1)
    z2 = jnp.zeros_like(cmp_w2)
    w2p = jnp.stack([jnp.concatenate([cmp_w2, z2], axis=2), jnp.concatenate([z2, cmp_w2], axis=2)], axis=1)
    return w1p.astype(BF16), w2p.astype(BF16)


def _pe_term(pe_ref, w1f_ref):
    pe = jnp.broadcast_to(pe_ref[...], (SUBLANES, CMP_L * NSA_DH))
    w = w1f_ref[...]
    wh, wm, wl = _split3(w)
    ph, pm, pl_ = _split3(pe)
    out = (_dot(ph, wh) + _dot(ph, wm) + _dot(pm, wh)
           + _dot(ph, wl) + _dot(pm, wm) + _dot(pl_, wh))
    return out[0:1, :]


def _cmp_prompt_kernel(x_ref, w1p_ref, w2p_ref, pe_ref, w1f_ref, o_ref):
    n_sub = x_ref.shape[0] // CMP_D
    pe_t = _pe_term(pe_ref, w1f_ref)
    out = jnp.zeros((n_sub, LANES), F32)
    for par in range(2):
        acc = jnp.zeros((n_sub, CMP_R * CMP_HID), F32)
        for j in range(CMP_D):
            xj = x_ref[pl.ds(j, n_sub, stride=CMP_D), :].astype(BF16)
            acc = acc + _dot(xj, w1p_ref[par, j])
        nxt = pltpu.roll(acc[:, CMP_HID:], n_sub - 1, 0)
        hid = pe_t + acc[:, :CMP_HID] + nxt
        out = out + _dot(_silu(hid).astype(BF16), w2p_ref[par])
    o_ref[...] = out.astype(o_ref.dtype)


def _cmp_prompt(kv, w1p, w2p, cmp_w1, cmp_pe, b, t):
    assert t % CMP_D == 0 and (t // CMP_D) % SUBLANES == 0
    n_sub = t // CMP_D
    kv3 = kv.reshape(b, t, 6 * KV_DIM)
    pe = cmp_pe.reshape(2, 1, CMP_L * NSA_DH)
    w1f = cmp_w1.reshape(2, CMP_L * NSA_DH, CMP_HID)
    return pl.pallas_call(
        _cmp_prompt_kernel,
        grid=(b, 2, 2),
        in_specs=[pl.BlockSpec((None, t, LANES), lambda bi, w, p: (bi, 0, 2 * w + p)),
                  pl.BlockSpec((None, 2, CMP_D, LANES, CMP_R * CMP_HID), lambda bi, w, p: (w, 0, 0, 0, 0)),
                  pl.BlockSpec((None, 2, CMP_HID, LANES), lambda bi, w, p: (w, 0, 0, 0)),
                  pl.BlockSpec((None, 1, CMP_L * NSA_DH), lambda bi, w, p: (w, 0, 0)),
                  pl.BlockSpec((None, CMP_L * NSA_DH, CMP_HID), lambda bi, w, p: (w, 0, 0))],
        out_specs=pl.BlockSpec((None, None, n_sub, LANES), lambda bi, w, p: (bi, w, 0, p)),
        out_shape=jax.ShapeDtypeStruct((b, 2, n_sub, KV_DIM), BF16),
        compiler_params=_cparams(("arbitrary", "arbitrary", "arbitrary")),
        name="cmp_prompt",
    )(kv3, w1p, w2p, pe, w1f)


TQ = 256
SLC_TK = 512


def _overlap_t(n_slc, n_cmp_pad):
    ci = np.arange(n_cmp_pad)[None, :] * CMP_D
    sj = np.arange(n_slc)[:, None] * SLC_L
    return ((ci < sj + SLC_L) & (ci + CMP_L > sj)).astype(np.float32)


def _gate_expand(pair):
    m = np.zeros((3, LANES, 2 * NSA_GROUP * NSA_DH), np.float32)
    for br in range(3):
        for hp in range(2):
            for g in range(NSA_GROUP):
                src = br * NSA_HEADS + (2 * pair + hp) * NSA_GROUP + g
                c0 = (hp * NSA_GROUP + g) * NSA_DH
                m[br, src, c0:c0 + NSA_DH] = 1.0
    return m


def _place_pair(x_even, x_odd, hp, lane):
    if hp == 0:
        return jnp.where(lane < NSA_DH, x_even, pltpu.roll(x_odd, NSA_DH, 1))
    return jnp.where(lane < NSA_DH, pltpu.roll(x_even, NSA_DH, 1), x_odd)


def _nsa_prompt_kernel(q_ref, sk_ref, sv_ref, wk_ref, wv_ref, kc_ref, vc_ref, gate_ref, gates_ref,
                       ovt_ref, gex_ref, o_ref, ka_ref, va_ref, wka_ref, wva_ref):
    i = pl.program_id(2)
    tq = q_ref.shape[0]
    t_len = sk_ref.shape[0]
    n_slc = ovt_ref.shape[0]
    n_cmp_pad = kc_ref.shape[0]
    rows = 2 * NSA_GROUP * tq
    n_sel = min(N_SEL, n_slc)

    @pl.when(i == 0)
    def _():
        ones = jnp.ones((t_len, LANES), BF16)
        jl = lax.broadcasted_iota(jnp.int32, (t_len, LANES), 1)
        ka_ref[:, 0:LANES] = sk_ref[...]
        kb = lax.broadcasted_iota(jnp.int32, (t_len, LANES), 0) // SLC_L
        ka_ref[:, LANES:2 * LANES] = jnp.where(kb == jl, 1.0, 0.0).astype(BF16)
        va_ref[0] = jnp.where(jl < NSA_DH, sv_ref[...], ones)
        va_ref[1] = jnp.where(jl < NSA_DH, ones, sv_ref[...])
        jp = lax.broadcasted_iota(jnp.int32, (WINDOW, 2 * LANES), 1)
        wka_ref[0:WINDOW, :] = jnp.where(jp == LANES, NEG, 0.0).astype(BF16)
        wka_ref[WINDOW:WINDOW + t_len, 0:LANES] = wk_ref[...]
        wka_ref[WINDOW:WINDOW + t_len, LANES:2 * LANES] = jnp.zeros((t_len, LANES), BF16)
        for hp in range(2):
            wva_ref[hp, 0:WINDOW, :] = jnp.zeros((WINDOW, LANES), BF16)
        wva_ref[0, WINDOW:WINDOW + t_len, :] = jnp.where(jl < NSA_DH, wv_ref[...], ones)
        wva_ref[1, WINDOW:WINDOW + t_len, :] = jnp.where(jl < NSA_DH, ones, wv_ref[...])

    lane = lax.broadcasted_iota(jnp.int32, (tq, LANES), 1)
    qpos_r = i * tq + lax.broadcasted_iota(jnp.int32, (rows, 1), 0) % tq
    sig_gates = jax.nn.sigmoid(gates_ref[...])

    qs = []
    for hp in range(2):
        keep = (lane >= NSA_DH) if hp == 1 else (lane < NSA_DH)
        for g in range(NSA_GROUP):
            blk = hp * 2 + g // 2
            x = q_ref[:, blk * LANES:(blk + 1) * LANES]
            if g % 2 != hp:
                x = pltpu.roll(x, NSA_DH, 1)
            qs.append(jnp.where(keep, x, jnp.zeros_like(x)))
    qs = jnp.concatenate(qs, axis=0) * jnp.asarray(NSA_DH ** -0.5, BF16)

    s = _dot_nt(qs, kc_ref[...])
    end = lax.broadcasted_iota(jnp.int32, (rows, n_cmp_pad), 1) * CMP_D + (CMP_L - 1)
    msk = end <= qpos_r
    s = jnp.where(msk, s, NEG)
    m = jnp.max(s, axis=-1, keepdims=True)
    p = jnp.where(msk, jnp.exp(s - m), 0.0)
    den = jnp.sum(p, axis=-1, keepdims=True)
    p = p / jnp.where(den > 0, den, 1.0)
    o_c = _dot(p.astype(BF16), vc_ref[...])

    jb = lax.broadcasted_iota(jnp.int32, (n_slc, tq), 0)
    qpos_l = i * tq + lax.broadcasted_iota(jnp.int32, (n_slc, tq), 1)
    cur = qpos_l // SLC_L
    forced = (jb == 0) | (jb == cur) | (jb == cur - 1)
    valid = jb * SLC_L <= qpos_l
    selfeat = []
    for hp in range(2):
        r0 = hp * NSA_GROUP * tq
        psum = p[r0:r0 + tq] + p[r0 + tq:r0 + 2 * tq] + p[r0 + 2 * tq:r0 + 3 * tq] + p[r0 + 3 * tq:r0 + 4 * tq]
        imp_t = _dot3_nt(ovt_ref[...], psum)
        score = jnp.where(forced, FORCED_SCORE, jnp.where(valid, imp_t, -1.0))
        cnt = jnp.zeros((n_slc, tq), jnp.int32)
        for jp in range(n_slc):
            r = score[jp:jp + 1, :]
            beats = (r > score) | ((r == score) & (jb > jp))
            cnt = cnt + beats.astype(jnp.int32)
        self_t = jnp.where(cnt < n_sel, 0.0, NEG)
        if n_slc < LANES:
            self_t = jnp.concatenate([self_t, jnp.zeros((LANES - n_slc, tq), F32)], axis=0)
        selfeat += [self_t.T.astype(BF16)] * NSA_GROUP
    qa = jnp.concatenate([qs, jnp.concatenate(selfeat, axis=0)], axis=1)

    tk = min(SLC_TK, t_len)

    hrows = rows // 2
    qa_h = (qa[:hrows], qa[hrows:])
    qpos_h = qpos_r[:hrows]

    def slc_tile(kt, carry, causal):
        ks = pl.ds(pl.multiple_of(kt * tk, tk), tk)
        out = []
        for hp in range(2):
            m_i, acc = carry[hp]
            s2 = _dot_nt(qa_h[hp], ka_ref[ks, :])
            if causal:
                kpos = kt * tk + lax.broadcasted_iota(jnp.int32, (hrows, tk), 1)
                s2 = jnp.where(kpos <= qpos_h, s2, NEG)
            m_n = jnp.maximum(m_i, jnp.max(s2, axis=-1, keepdims=True))
            al = jnp.exp(m_i - m_n)
            p2 = jnp.exp((s2 - m_n).astype(BF16))
            out.append((m_n, al * acc + _dot(p2, va_ref[hp, ks, :])))
        return tuple(out)

    lane_h = lax.broadcasted_iota(jnp.int32, (hrows, LANES), 1)

    def normalise(acc, hp):
        head = (lane_h >= NSA_DH) if hp == 1 else (lane_h < NSA_DH)
        return acc / jnp.where(head, pltpu.roll(acc, NSA_DH, 1), 1.0)

    n_full = (i * tq) // tk
    init = (jnp.full((hrows, 1), NEG, F32), jnp.zeros((hrows, LANES), F32))
    carry = lax.fori_loop(0, n_full, functools.partial(slc_tile, causal=False), (init, init))
    carry = slc_tile(n_full, carry, True)
    o_s = jnp.concatenate([normalise(carry[0][1], 0), normalise(carry[1][1], 1)], axis=0)

    lw = WINDOW + tq
    ws = pl.ds(pl.multiple_of(i * tq, tq), lw)
    qw = jnp.concatenate([qs, jnp.where(lax.broadcasted_iota(jnp.int32, (rows, LANES), 1) == 0, 1.0, 0.0
                                        ).astype(BF16)], axis=1)
    cr = (lax.broadcasted_iota(jnp.int32, (hrows, tq), 1)
          - lax.broadcasted_iota(jnp.int32, (hrows, tq), 0) % tq)
    o_w = []
    for hp in range(2):
        s3 = _dot_nt(qw[hp * hrows:(hp + 1) * hrows], wka_ref[ws, :])
        s3 = jnp.concatenate([jnp.where(cr > 0, s3[:, :tq], NEG), s3[:, tq:lw - tq],
                              jnp.where(cr <= 0, s3[:, lw - tq:], NEG)], axis=1)
        m3 = jnp.max(s3, axis=-1, keepdims=True)
        p3 = jnp.exp((s3 - m3).astype(BF16))
        o_w.append(normalise(_dot(p3, wva_ref[hp, ws, :]), hp))
    o_w = jnp.concatenate(o_w, axis=0)

    sig_hi = sig_gates.astype(BF16)
    sig_lo = (sig_gates - sig_hi.astype(F32)).astype(BF16)
    out = None
    for br, ob in enumerate((o_c, o_s, o_w)):
        parts = []
        for hp in range(2):
            for gp in range(2):
                r0 = (hp * NSA_GROUP + 2 * gp) * tq
                parts.append(_place_pair(ob[r0:r0 + tq], ob[r0 + tq:r0 + 2 * tq], hp, lane))
        gexp = _dot(sig_hi, gex_ref[br]) + _dot(sig_lo, gex_ref[br])
        term = gexp * jnp.concatenate(parts, axis=1)
        out = term if out is None else out + term
    o_ref[...] = (out * _silu(gate_ref[...])).astype(o_ref.dtype)


def _nsa_prompt(q, kvb, kcv, g, b, t):
    assert t % TQ == 0 and t % SLC_L == 0 and (t <= SLC_TK or t % SLC_TK == 0)
    n_sub = t // CMP_D
    n_slc = t // SLC_L
    assert n_slc <= LANES
    q3 = q.reshape(b, t, D_NSA)
    kvb3 = kvb.reshape(b, t, 6 * KV_DIM)
    g3 = g.reshape(b, t, G_WIDTH)
    ovt = jnp.asarray(_overlap_t(n_slc, n_sub), BF16)
    gex = jnp.asarray(np.stack([_gate_expand(0), _gate_expand(1)]), BF16)
    pw = 2 * NSA_GROUP * NSA_DH

    def kvcol(which):
        return pl.BlockSpec((None, t, LANES), lambda bi, p, i: (bi, 0, 2 * which + p))

    return pl.pallas_call(
        _nsa_prompt_kernel,
        grid=(b, 2, t // TQ),
        in_specs=[pl.BlockSpec((None, TQ, pw), lambda bi, p, i: (bi, i, p)),
                  kvcol(2), kvcol(3), kvcol(4), kvcol(5),
                  pl.BlockSpec((None, None, n_sub, LANES), lambda bi, p, i: (bi, 0, 0, p)),
                  pl.BlockSpec((None, None, n_sub, LANES), lambda bi, p, i: (bi, 1, 0, p)),
                  pl.BlockSpec((None, TQ, pw), lambda bi, p, i: (bi, i, p)),
                  pl.BlockSpec((None, TQ, LANES), lambda bi, p, i: (bi, i, D_NSA // LANES)),
                  pl.BlockSpec(ovt.shape, lambda bi, p, i: (0, 0)),
                  pl.BlockSpec((None, 3, LANES, pw), lambda bi, p, i: (p, 0, 0, 0))],
        out_specs=pl.BlockSpec((None, TQ, pw), lambda bi, p, i: (bi, i, p)),
        out_shape=jax.ShapeDtypeStruct((b, t, D_NSA), BF16),
        scratch_shapes=[pltpu.VMEM((t, 2 * LANES), BF16), pltpu.VMEM((2, t, LANES), BF16),
                        pltpu.VMEM((t + WINDOW, 2 * LANES), BF16), pltpu.VMEM((2, t + WINDOW, LANES), BF16)],
        compiler_params=_cparams(("arbitrary", "arbitrary", "arbitrary")),
        name="nsa_prompt",
    )(q3, kvb3, kvb3, kvb3, kvb3, kcv, kcv, g3, g3, ovt, gex)


CMP_PAGES = 32


def _cmp_bd_weights(cmp_w1, cmp_w2):
    w1 = cmp_w1.reshape(2, CMP_R, CMP_D, NSA_DH, CMP_HID)
    w1 = jnp.transpose(w1, (0, 2, 3, 1, 4)).reshape(2, CMP_D, NSA_DH, CMP_R * CMP_HID)
    eye2 = jnp.eye(2, dtype=w1.dtype)
    bd = jnp.einsum('wjdn,hg->wjhdgn', w1, eye2).reshape(2, CMP_D, LANES, 2 * CMP_R * CMP_HID)
    eye4 = jnp.eye(NSA_KV_HEADS, dtype=w1.dtype)
    w2q = jnp.einsum('wed,hg->whegd', cmp_w2, eye4).reshape(2, NSA_KV_HEADS, CMP_HID, KV_DIM)
    return bd.astype(BF16), w2q.astype(BF16)


def _cmp_sample_kernel(pt_ref, cache_ref, perm_ref, bd_ref, w2q_ref, pe_ref, w1f_ref, o_ref, tbuf_ref, buf_ref,
                       acc_ref, sem_ref, *, pages, n_b, n_c):
    w = pl.program_id(0)
    b = pl.program_id(1)
    c = pl.program_id(2)
    step = (w * n_b + b) * n_c + c
    total = 2 * n_b * n_c
    slot = step % 2
    rows = pages * PAGE_SIZE // CMP_D

    def copies(w_, b_, c_, slot_):
        return [pltpu.make_async_copy(
            cache_ref.at[pt_ref[b_, c_ * pages + p], pl.ds(pl.multiple_of(w_ * KV_DIM, KV_DIM), KV_DIM), :],
            tbuf_ref.at[slot_, p], sem_ref.at[slot_]) for p in range(pages)]

    @pl.when(step == 0)
    def _():
        for cp in copies(w, b, c, slot):
            cp.start()

    @pl.when(step + 1 < total)
    def _():
        nxt = step + 1
        c_n = nxt % n_c
        b_n = (nxt // n_c) % n_b
        w_n = nxt // (n_c * n_b)
        for cp in copies(w_n, b_n, c_n, 1 - slot):
            cp.start()

    for cp in copies(w, b, c, slot):
        cp.wait()

    def to_rows(p, carry):
        xp = _dot_nt(perm_ref[...], tbuf_ref[slot, p].astype(BF16))
        for half in range(2):
            buf_ref[half, p] = xp[:, half * LANES:(half + 1) * LANES].reshape(CMP_D, n_pp, LANES)
        return carry

    n_pp = PAGE_SIZE // CMP_D
    lax.fori_loop(0, pages, to_rows, 0, unroll=8 if pages % 8 == 0 else 1)

    pw = 2 * CMP_R * CMP_HID
    for half in range(2):
        acc = jnp.zeros((rows, pw), F32)
        for jp in range(CMP_D // 2):
            xa = buf_ref[half, :, 2 * jp].reshape(rows, LANES)
            xb = buf_ref[half, :, 2 * jp + 1].reshape(rows, LANES)
            acc = acc + _dot(jnp.concatenate([xa, xb], axis=1).astype(BF16), bd_ref[jp])
        acc_ref[pl.ds(pl.multiple_of(c * rows, rows), rows), half * pw:(half + 1) * pw] = acc

    @pl.when(c == n_c - 1)
    def _():
        n_sub = acc_ref.shape[0]
        pe_t = _pe_term(pe_ref, w1f_ref)
        out = jnp.zeros((n_sub, KV_DIM), F32)
        for h in range(NSA_KV_HEADS):
            c0 = h * CMP_R * CMP_HID
            nxt = pltpu.roll(acc_ref[:, c0 + CMP_HID:c0 + 2 * CMP_HID], n_sub - 1, 0)
            hid = pe_t + acc_ref[:, c0:c0 + CMP_HID] + nxt
            out = out + _dot(_silu(hid).astype(BF16), w2q_ref[h])
        o_ref[...] = out.astype(o_ref.dtype)


def _cmp_sample(cache_t, page_table, bd, w2q, cmp_w1, cmp_pe):
    n_b, n_pages = page_table.shape
    pages = min(CMP_PAGES, n_pages)
    assert n_pages % pages == 0
    n_c = n_pages // pages
    n_sub = n_pages * PAGE_SIZE // CMP_D
    pe = cmp_pe.reshape(2, 1, CMP_L * NSA_DH)
    w1f = cmp_w1.reshape(2, CMP_L * NSA_DH, CMP_HID)
    nbd = NSA_KV_HEADS * CMP_R * CMP_HID
    n_pp = PAGE_SIZE // CMP_D
    perm = np.zeros((PAGE_SIZE, PAGE_SIZE), np.float32)
    jn = np.arange(PAGE_SIZE)
    perm[jn, (jn % n_pp) * CMP_D + jn // n_pp] = 1.0
    perm = jnp.asarray(perm, BF16)
    bd2 = bd.reshape(2, CMP_D // 2, 2 * LANES, nbd // 2)
    kern = functools.partial(_cmp_sample_kernel, pages=pages, n_b=n_b, n_c=n_c)
    return pl.pallas_call(
        kern,
        grid_spec=pltpu.PrefetchScalarGridSpec(
            num_scalar_prefetch=1,
            grid=(2, n_b, n_c),
            in_specs=[pl.BlockSpec(memory_space=pl.ANY),
                      pl.BlockSpec((PAGE_SIZE, PAGE_SIZE), lambda w, b, c, pt: (0, 0)),
                      pl.BlockSpec((None, CMP_D // 2, 2 * LANES, nbd // 2), lambda w, b, c, pt: (w, 0, 0, 0)),
                      pl.BlockSpec((None, NSA_KV_HEADS, CMP_HID, KV_DIM), lambda w, b, c, pt: (w, 0, 0, 0)),
                      pl.BlockSpec((None, 1, CMP_L * NSA_DH), lambda w, b, c, pt: (w, 0, 0)),
                      pl.BlockSpec((None, CMP_L * NSA_DH, CMP_HID), lambda w, b, c, pt: (w, 0, 0))],
            out_specs=pl.BlockSpec((None, None, n_sub, KV_DIM), lambda w, b, c, pt: (b, w, 0, 0)),
            scratch_shapes=[pltpu.VMEM((2, pages, KV_DIM, PAGE_SIZE), F32),
                            pltpu.VMEM((2, pages, CMP_D, n_pp, LANES), F32),
                            pltpu.VMEM((n_sub, nbd), F32),
                            pltpu.SemaphoreType.DMA((2,))]),
        out_shape=jax.ShapeDtypeStruct((n_b, 2, n_sub, KV_DIM), BF16),
        compiler_params=_cparams(("arbitrary", "arbitrary", "arbitrary")),
        name="cmp_sample",
    )(page_table, cache_t, perm, bd2, w2q, pe, w1f)


def _nsa_sample_a_kernel(qh_ref, kc_ref, vc_ref, ov_ref, oc_ref, sel_ref, *, past_len, n_slc):
    n_sub = kc_ref.shape[0]
    n_slc_pad = ov_ref.shape[1]
    n_sel = min(N_SEL, n_slc)
    qpos = past_len
    rowi = lax.broadcasted_iota(jnp.int32, (SUBLANES, n_sub), 0)
    rowq = lax.broadcasted_iota(jnp.int32, (SROWS, n_sub), 0)
    ncol = lax.broadcasted_iota(jnp.int32, (SROWS, n_sub), 1)
    msk = (ncol * CMP_D + (CMP_L - 1) <= qpos) & (ncol < n_sub - 1)
    psum = jnp.zeros((SUBLANES, n_sub), F32)
    for h in range(NSA_KV_HEADS):
        pr = h // 2
        qh = (qh_ref[h] * NSA_DH ** -0.5).astype(BF16)
        s = _dot_nt(qh, kc_ref[:, pr * LANES:(pr + 1) * LANES])
        s = jnp.where(msk, s, NEG)
        m = jnp.max(s, axis=-1, keepdims=True)
        p = jnp.where(msk, jnp.exp(s - m), 0.0)
        den = jnp.sum(p, axis=-1, keepdims=True)
        p = p / jnp.where(den > 0, den, 1.0)
        oc_ref[h] = _dot(p.astype(BF16), vc_ref[:, pr * LANES:(pr + 1) * LANES])
        ph = jnp.sum(jnp.where(rowq < NSA_GROUP, p, 0.0), axis=0, keepdims=True)
        psum = jnp.where(rowi == h, ph, psum)

    hi, mid, lo = _split3(psum)
    imp = _dot(hi, ov_ref[...]) + _dot(mid, ov_ref[...]) + _dot(lo, ov_ref[...])
    jb = lax.broadcasted_iota(jnp.int32, (SUBLANES, n_slc_pad), 1)
    cur = qpos // SLC_L
    forced = (jb == 0) | (jb == cur) | (jb == cur - 1)
    valid = jb * SLC_L <= qpos
    score = jnp.where(forced, FORCED_SCORE, jnp.where(valid, imp, -1.0))
    score = jnp.where(jb < n_slc, score, -2.0)
    cnt = jnp.zeros((SUBLANES, n_slc_pad), jnp.int32)
    for jp in range(n_slc):
        r = score[:, jp:jp + 1]
        beats = (r > score) | ((r == score) & (jb > jp))
        cnt = cnt + beats.astype(jnp.int32)
    cnt = jnp.where(jb < n_slc, cnt, n_slc_pad)
    lane = lax.broadcasted_iota(jnp.int32, (SUBLANES, LANES), 1)
    idx = jnp.zeros((SUBLANES, LANES), jnp.int32)
    for slot in range(n_sel):
        pick = jnp.sum(jnp.where(cnt == slot, jb, 0), axis=1, keepdims=True)
        idx = jnp.where(lane == slot, pick, idx)
    sel_ref[...] = idx


def _nsa_sample_a(qh, kcv, past_len):
    n_b = qh.shape[0]
    n_sub = kcv.shape[2]
    n_slc = -(-(past_len + 1) // SLC_L)
    n_slc_pad = -(-n_slc // LANES) * LANES
    ov = np.zeros((n_sub, n_slc_pad), np.float32)
    ov[:, :n_slc] = _overlap_t(n_slc, n_sub).T
    ov = jnp.asarray(ov, BF16)
    kern = functools.partial(_nsa_sample_a_kernel, past_len=past_len, n_slc=n_slc)
    return pl.pallas_call(
        kern,
        grid=(n_b,),
        in_specs=[pl.BlockSpec((None, NSA_KV_HEADS, SROWS, LANES), lambda b: (b, 0, 0, 0)),
                  pl.BlockSpec((None, None, n_sub, KV_DIM), lambda b: (b, 0, 0, 0)),
                  pl.BlockSpec((None, None, n_sub, KV_DIM), lambda b: (b, 1, 0, 0)),
                  pl.BlockSpec(ov.shape, lambda b: (0, 0))],
        out_specs=[pl.BlockSpec((None, NSA_KV_HEADS, SROWS, LANES), lambda b: (b, 0, 0, 0)),
                   pl.BlockSpec((None, SUBLANES, LANES), lambda b: (b, 0, 0))],
        out_shape=[jax.ShapeDtypeStruct((n_b, NSA_KV_HEADS, SROWS, LANES), F32),
                   jax.ShapeDtypeStruct((n_b, SUBLANES, LANES), jnp.int32)],
        compiler_params=_cparams(("arbitrary",)),
        name="nsa_sample_a",
    )(qh, kcv, kcv, ov), n_slc


def _nsa_sample_b_kernel(pt_ref, sel_ref, qh_ref, cache_ref, new_ref, win_ref, oc_ref, gates_ref, gate_ref,
                         o_ref, kbuf_ref, vbuf_ref, sem_ref, *, past_len, n_slc, n_sel, n_b):
    b = pl.program_id(0)
    slot = b % 2
    n_blocks_cached = past_len // SLC_L
    per_page = PAGE_SIZE // SLC_L

    def copies(b_, slot_):
        out = []
        for h in range(NSA_KV_HEADS):
            for k in range(n_sel):
                j = jnp.minimum(sel_ref[b_, h, k], n_blocks_cached - 1)
                page = pt_ref[b_, j // per_page]
                for which, buf in ((2, kbuf_ref), (3, vbuf_ref)):
                    out.append(pltpu.make_async_copy(
                        cache_ref.at[page, pl.ds(which * KV_DIM + (h // 2) * LANES, LANES), :],
                        buf.at[slot_, h, :, pl.ds(k * PAGE_SIZE, PAGE_SIZE)],
                        sem_ref.at[slot_]))
        return out

    @pl.when(b == 0)
    def _():
        for cp in copies(b, slot):
            cp.start()

    @pl.when(b + 1 < n_b)
    def _():
        for cp in copies(b + 1, 1 - slot):
            cp.start()

    for cp in copies(b, slot):
        cp.wait()

    qpos = past_len
    nk = n_sel * PAGE_SIZE
    lane_k = lax.broadcasted_iota(jnp.int32, (1, nk), 1)
    n_buf = win_ref.shape[1]
    widx = lax.broadcasted_iota(jnp.int32, (1, n_buf), 1)
    rel = n_buf - widx
    wmask = (rel >= 0) & (rel < WINDOW) & (past_len - n_buf + widx >= 0)
    rowg = lax.broadcasted_iota(jnp.int32, (SROWS, LANES), 0)

    for h in range(NSA_KV_HEADS):
        pr = h // 2
        ls = slice(pr * LANES, (pr + 1) * LANES)
        qf = qh_ref[h] * NSA_DH ** -0.5
        qh = qf.astype(BF16)

        def new_key(row):
            return new_ref[row:row + 1, ls].astype(BF16).astype(F32)

        blk = jnp.zeros((1, nk), jnp.int32)
        n_cur = jnp.int32(0)
        for k in range(n_sel):
            sj = sel_ref[b, h, k]
            blk = blk + jnp.where(lane_k // PAGE_SIZE == k, sj, 0)
            n_cur = n_cur + (sj == qpos // SLC_L).astype(jnp.int32)
        has_new = (jnp.zeros((1, 1), jnp.int32) + n_cur) > 0
        kvalid = ((lane_k % PAGE_SIZE) // SLC_L == blk % per_page) & (blk < n_blocks_cached)
        s = _dot(qh, kbuf_ref[slot, h].astype(BF16))
        s = jnp.where(kvalid, s, NEG)
        s_new = jnp.sum(qf * new_key(2), axis=-1, keepdims=True)
        s_new = jnp.where(has_new, s_new, NEG)
        m = jnp.maximum(jnp.max(s, axis=-1, keepdims=True), s_new)
        p = jnp.where(kvalid, jnp.exp(s - m), 0.0)
        p_new = jnp.where(has_new, jnp.exp(s_new - m), 0.0)
        den = jnp.sum(p, axis=-1, keepdims=True) + p_new
        den = jnp.where(den > 0, den, 1.0)
        o_s = (_dot_nt(p.astype(BF16), vbuf_ref[slot, h].astype(BF16)) + p_new * new_key(3)) / den

        s = _dot(qh, win_ref[pr * LANES:(pr + 1) * LANES, :].astype(BF16))
        s = jnp.where(wmask, s, NEG)
        s_new = jnp.sum(qf * new_key(4), axis=-1, keepdims=True)
        m = jnp.maximum(jnp.max(s, axis=-1, keepdims=True), s_new)
        p = jnp.where(wmask, jnp.exp(s - m), 0.0)
        p_new = jnp.exp(s_new - m)
        den = jnp.sum(p, axis=-1, keepdims=True) + p_new
        wv = win_ref[KV_DIM + pr * LANES:KV_DIM + (pr + 1) * LANES, :].astype(BF16)
        o_w = (_dot_nt(p.astype(BF16), wv) + p_new * new_key(5)) / den

        g = jax.nn.sigmoid(gates_ref[:, h])
        o = g[0] * oc_ref[h] + g[1] * o_s + g[2] * o_w
        o_ref[h] = jnp.where(rowg < NSA_GROUP, o * _silu(gate_ref[h]), 0.0)


def _nsa_sample_b(page_table, sel, qh, cache_t, kv_new, win_t, o_c, gates_l, gate_l, past_len, n_slc):
    n_b = qh.shape[0]
    n_sel = sel.shape[2]
    n_buf = win_t.shape[2]
    kern = functools.partial(_nsa_sample_b_kernel, past_len=past_len, n_slc=n_slc, n_sel=n_sel, n_b=n_b)
    hb = (None, NSA_KV_HEADS, SROWS, LANES)
    return pl.pallas_call(
        kern,
        grid_spec=pltpu.PrefetchScalarGridSpec(
            num_scalar_prefetch=2,
            grid=(n_b,),
            in_specs=[pl.BlockSpec(hb, lambda b, pt, sl: (b, 0, 0, 0)),
                      pl.BlockSpec(memory_space=pl.ANY),
                      pl.BlockSpec((None, 6, KV_DIM), lambda b, pt, sl: (b, 0, 0)),
                      pl.BlockSpec((None, 2 * KV_DIM, n_buf), lambda b, pt, sl: (b, 0, 0)),
                      pl.BlockSpec(hb, lambda b, pt, sl: (b, 0, 0, 0)),
                      pl.BlockSpec((None, 3, NSA_KV_HEADS, SROWS, LANES), lambda b, pt, sl: (b, 0, 0, 0, 0)),
                      pl.BlockSpec(hb, lambda b, pt, sl: (b, 0, 0, 0))],
            out_specs=pl.BlockSpec(hb, lambda b, pt, sl: (b, 0, 0, 0)),
            scratch_shapes=[pltpu.VMEM((2, NSA_KV_HEADS, LANES, n_sel * PAGE_SIZE), F32),
                            pltpu.VMEM((2, NSA_KV_HEADS, LANES, n_sel * PAGE_SIZE), F32),
                            pltpu.SemaphoreType.DMA((2,))]),
        out_shape=jax.ShapeDtypeStruct((n_b, NSA_KV_HEADS, SROWS, LANES), F32),
        compiler_params=_cparams(("arbitrary",)),
        name="nsa_sample_b",
    )(page_table, sel, qh, cache_t, kv_new, win_t, o_c, gates_l, gate_l)


def _head_layout(x):
    n_b = x.shape[0]
    x = x.astype(F32)
    z = jnp.zeros_like(x)
    even = jnp.concatenate([x, z], axis=-1)
    odd = jnp.concatenate([z, x], axis=-1)
    par = (jnp.arange(NSA_KV_HEADS) % 2).reshape(1, NSA_KV_HEADS, 1, 1)
    y = jnp.where(par == 0, even, odd)
    return jnp.concatenate([y, jnp.zeros((n_b, NSA_KV_HEADS, SROWS - NSA_GROUP, LANES), F32)], axis=2)


def _sample_path(x_sample, cache_kv, cache_win, state, page_table, w_p, w_out_b, hg_lb, norm_g,
                 cmp_w1, cmp_w2, cmp_pe, ln_g, ln_b):
    n_b, t_s, _ = x_sample.shape
    assert t_s == 1, "single-token decode"
    n_pool = cache_kv.shape[0]
    n_pages = page_table.shape[1]
    past_len = n_pages * PAGE_SIZE
    xs = x_sample.reshape(n_b, D_MODEL)
    hg, q, kv, _, g = _inproj(xs, w_p, n_b)
    o_hg, s_new = _hgrn_sample(hg, state, hg_lb, norm_g)

    cache_t = jnp.transpose(cache_kv, (0, 2, 3, 4, 1)).reshape(n_pool, 4 * KV_DIM, PAGE_SIZE)
    bd, w2q = _cmp_bd_weights(cmp_w1, cmp_w2)
    kcv = _cmp_sample(cache_t, page_table, bd, w2q, cmp_w1, cmp_pe)

    qh = _head_layout(q.reshape(n_b, NSA_KV_HEADS, NSA_GROUP, NSA_DH))
    (o_c, sel), n_slc = _nsa_sample_a(qh, kcv, past_len)
    n_sel = min(N_SEL, n_slc)
    sel = sel[:, :NSA_KV_HEADS, :n_sel]

    gates = g[:, D_NSA:D_NSA + 3 * NSA_HEADS].reshape(n_b, 3, NSA_KV_HEADS, NSA_GROUP, 1)
    gates_l = _head_layout_b(jnp.broadcast_to(gates, (n_b, 3, NSA_KV_HEADS, NSA_GROUP, NSA_DH)))
    gate_l = _head_layout(g[:, :D_NSA].reshape(n_b, NSA_KV_HEADS, NSA_GROUP, NSA_DH))
    n_buf = cache_win.shape[1]
    win_t = jnp.transpose(cache_win, (0, 2, 3, 4, 1)).reshape(n_b, 2 * KV_DIM, n_buf)
    o = _nsa_sample_b(page_table, sel, qh, cache_t, kv.reshape(n_b, 6, KV_DIM), win_t, o_c, gates_l, gate_l,
                      past_len, n_slc)
    o = o.reshape(n_b, NSA_KV_HEADS, SROWS, 2, NSA_DH)[:, :, :NSA_GROUP].sum(axis=3)
    o_nsa = o.reshape(n_b, D_NSA).astype(BF16)

    y = _outproj(xs, o_hg.reshape(n_b, D_HG).astype(BF16), o_nsa, w_out_b, ln_g, ln_b, n_b)
    kv6 = kv.reshape(n_b, 1, 6, NSA_KV_HEADS, NSA_DH)
    win_cat = jnp.concatenate([cache_win, kv6[:, :, 4:].astype(cache_win.dtype)], axis=1)
    return (y.reshape(n_b, 1, D_MODEL), kv6[None, :, :, :4].astype(cache_kv.dtype), win_cat[None, :, 1:],
            s_new[None].astype(state.dtype))


def _head_layout_b(x):
    n_b = x.shape[0]
    y = _head_layout(x.reshape(n_b * 3, NSA_KV_HEADS, NSA_GROUP, NSA_DH))
    return y.reshape(n_b, 3, NSA_KV_HEADS, SROWS, LANES)


def kernel(x_prompt, x_sample, cache_kv, cache_win, state_hgrn, page_table, w_in, hg_lb, hg_norm_g,
           cmp_w1, cmp_w2, cmp_pe, w_out, ln_g, ln_b):
    assert w_in.shape[0] == 1, "single-layer decoder"
    b, t, _ = x_prompt.shape
    w_p = _prep_w_in(w_in[0])
    w_out_b = w_out[0].astype(BF16)
    w1p, w2p = _cmp_weights(cmp_w1[0], cmp_w2[0])

    xp = x_prompt.reshape(b * t, D_MODEL)
    tm = 1024 if (b * t) % 1024 == 0 else 256
    hg, q, kv, kvb, g = _inproj(xp, w_p, tm)
    o_hg, s_p = _hgrn_prompt(hg, hg_lb, hg_norm_g[0], b, t)
    kcv = _cmp_prompt(kv, w1p, w2p, cmp_w1[0], cmp_pe[0], b, t)
    o_nsa = _nsa_prompt(q, kvb, kcv, g, b, t)
    y_p = _outproj(xp, o_hg.reshape(b * t, D_HG), o_nsa.reshape(b * t, D_NSA), w_out_b,
                   ln_g[0], ln_b[0], 512 if (b * t) % 512 == 0 else 256)
    kv6 = kv.reshape(b, t, 6, NSA_KV_HEADS, NSA_DH)
    wlen = min(WINDOW, t)

    y_s, kv_s, win_s, s_s = _sample_path(x_sample, cache_kv[0], cache_win[0], state_hgrn[0], page_table, w_p,
                                         w_out_b, hg_lb, hg_norm_g[0], cmp_w1[0], cmp_w2[0], cmp_pe[0],
                                         ln_g[0], ln_b[0])
    return (y_p.reshape(b, t, D_MODEL), y_s, kv6[None, :, :, :4], kv6[None, :, t - wlen:, 4:],
            s_p[None].astype(x_prompt.dtype), kv_s, win_s, s_s)
```

```python
import functools

import numpy as np
import jax
import jax.numpy as jnp
from jax import lax
from jax.experimental import pallas as pl
from jax.experimental.pallas import tpu as pltpu

F32 = jnp.float32
BF16 = jnp.bfloat16

D_MODEL = 2048
D_HG = 1024
D_NSA = 1024
HG_DK = 128
HG_DV = 128
HG_HEADS = 8
NSA_DH = 64
NSA_HEADS = 16
NSA_KV_HEADS = 4
NSA_GROUP = 4
KV_DIM = 256
CMP_L = 32
CMP_D = 16
CMP_R = 2
CMP_HID = 128
SLC_L = 64
N_SEL = 16
WINDOW = 512
PAGE_SIZE = 128
ALPHA = 2.0 ** 0.25
LN_EPS = 1e-5
RMS_EPS = 1e-6
FORCED_SCORE = 1e4
NEG = -1e30

LANES = 128
SUBLANES = 8
VMEM_LIMIT = 56 * 1024 * 1024
SROWS = 16

COL_TILE = 512
N_HG_TILES = 4 * D_HG // COL_TILE
N_Q_TILES = D_NSA // COL_TILE
N_KV_TILES = 6 * KV_DIM // COL_TILE
G_WIDTH = 1536
N_G_TILES = G_WIDTH // COL_TILE
N_COL_TILES = N_HG_TILES + N_Q_TILES + N_KV_TILES + N_G_TILES
D_IN_PAD = N_COL_TILES * COL_TILE

HG_CHUNK = 128
HG_DIAG = 8
HG_HALVES = tuple(HG_DIAG * 2 ** i for i in range((HG_CHUNK // HG_DIAG).bit_length() - 1))
HG_HPS = 2


def _cparams(sem, flags=None):
    return pltpu.CompilerParams(dimension_semantics=sem, vmem_limit_bytes=VMEM_LIMIT, flags=flags)


def _dot(a, b):
    return jnp.dot(a, b, preferred_element_type=F32)


def _dot_nt(a, b):
    return lax.dot_general(a, b, (((1,), (1,)), ((), ())), preferred_element_type=F32)


def _split3(x):
    hi = x.astype(BF16)
    r1 = x - hi.astype(F32)
    mid = r1.astype(BF16)
    lo = (r1 - mid.astype(F32)).astype(BF16)
    return hi, mid, lo


def _dot3(a_bf, x):
    hi, mid, lo = _split3(x)
    return _dot(a_bf, hi) + _dot(a_bf, mid) + _dot(a_bf, lo)


def _dot3_nt(a_bf, x):
    hi, mid, lo = _split3(x)
    return _dot_nt(a_bf, hi) + _dot_nt(a_bf, mid) + _dot_nt(a_bf, lo)


def _silu(x):
    return x * jax.nn.sigmoid(x)


def _inproj_kernel(x_ref, w_ref, hg_ref, q_ref, kv_ref, kvb_ref, g_ref, *rest, kv_t):
    kvt_ref, xb_ref = rest if kv_t else (None, rest[0])
    j = pl.program_id(1)

    @pl.when(j == 0)
    def _():
        xb_ref[...] = x_ref[...].astype(BF16)

    q0 = N_HG_TILES
    kv0 = q0 + N_Q_TILES
    g0 = kv0 + N_KV_TILES
    tm = x_ref.shape[0]
    n_parts = 2 if tm % (2 * LANES) == 0 else 1
    parts = [slice(h * (tm // n_parts), (h + 1) * (tm // n_parts)) for h in range(n_parts)]

    def emit(store):
        for rs in parts:
            store(rs, _dot(xb_ref[rs, :], w_ref[...]))

    @pl.when(j < q0)
    def _():
        def store(rs, acc):
            hg_ref[rs, :] = acc
        emit(store)

    @pl.when((j >= q0) & (j < kv0))
    def _():
        def store(rs, acc):
            q_ref[rs, :] = acc.astype(BF16)
        emit(store)

    @pl.when((j >= kv0) & (j < g0))
    def _():
        def store(rs, acc):
            kvb_ref[rs, :] = acc.astype(BF16)
            if kv_t:
                kvt_ref[:, rs] = acc.T

                @pl.when(j == kv0)
                def _():
                    kv_ref[rs, :] = acc
            else:
                kv_ref[rs, :] = acc
        emit(store)

    @pl.when(j >= g0)
    def _():
        def store(rs, acc):
            g_ref[rs, :] = acc
        emit(store)


def _inproj(x2d, w_p, tm, t_rows=None):
    m = x2d.shape[0]
    assert m % tm == 0
    kv_t = t_rows is not None
    q0 = N_HG_TILES
    kv0 = q0 + N_Q_TILES
    g0 = kv0 + N_KV_TILES

    def clampmap(lo, n):
        return lambda i, j: (i, jnp.clip(j - lo, 0, n - 1))

    out_specs = [pl.BlockSpec((tm, COL_TILE), clampmap(0, N_HG_TILES)),
                 pl.BlockSpec((tm, COL_TILE), clampmap(q0, N_Q_TILES)),
                 pl.BlockSpec((tm, COL_TILE), clampmap(kv0, 1 if kv_t else N_KV_TILES)),
                 pl.BlockSpec((tm, COL_TILE), clampmap(kv0, N_KV_TILES)),
                 pl.BlockSpec((tm, COL_TILE), clampmap(g0, N_G_TILES))]
    out_shape = [jax.ShapeDtypeStruct((m, 4 * D_HG), F32),
                 jax.ShapeDtypeStruct((m, D_NSA), BF16),
                 jax.ShapeDtypeStruct((m, COL_TILE if kv_t else 6 * KV_DIM), F32),
                 jax.ShapeDtypeStruct((m, 6 * KV_DIM), BF16),
                 jax.ShapeDtypeStruct((m, G_WIDTH), F32)]
    if kv_t:
        assert t_rows % tm == 0
        tpb = t_rows // tm
        out_specs.append(pl.BlockSpec((None, COL_TILE, tm),
                                      lambda i, j: (i // tpb, jnp.clip(j - kv0, 0, N_KV_TILES - 1), i % tpb)))
        out_shape.append(jax.ShapeDtypeStruct((m // t_rows, 6 * KV_DIM, t_rows), F32))
    return pl.pallas_call(
        functools.partial(_inproj_kernel, kv_t=kv_t),
        grid=(m // tm, N_COL_TILES),
        in_specs=[pl.BlockSpec((tm, D_MODEL), lambda i, j: (i, 0)),
                  pl.BlockSpec((D_MODEL, COL_TILE), lambda i, j: (0, j))],
        out_specs=out_specs,
        out_shape=out_shape,
        scratch_shapes=[pltpu.VMEM((tm, D_MODEL), BF16)],
        compiler_params=_cparams(("arbitrary", "arbitrary")),
        name="inproj",
    )(x2d, w_p)


def _prep_w_in(w_in):
    o_nq = 4 * D_HG
    o_kv = o_nq + D_NSA
    o_gates = o_kv + 6 * KV_DIM
    o_gate = o_gates + 3 * NSA_HEADS
    pad = G_WIDTH - D_NSA - 3 * NSA_HEADS
    w = jnp.concatenate([w_in[:, :o_gates], w_in[:, o_gate:], w_in[:, o_gates:o_gate],
                         jnp.zeros((D_MODEL, pad), w_in.dtype)], axis=1)
    return w.astype(BF16)


def _outproj_kernel(x_ref, ohg_ref, onsa_ref, w_ref, g_ref, b_ref, y_ref):
    y = _dot(ohg_ref[...], w_ref[0:D_HG, :]) + _dot(onsa_ref[...], w_ref[D_HG:D_MODEL, :])
    z = ALPHA * x_ref[...] + y
    mu = jnp.mean(z, axis=-1, keepdims=True)
    zc = z - mu
    var = jnp.mean(zc * zc, axis=-1, keepdims=True)
    y_ref[...] = zc * lax.rsqrt(var + LN_EPS) * g_ref[...] + b_ref[...]


def _outproj(x2d, o_hg, o_nsa, w_out_b, ln_g, ln_b, tm):
    m = x2d.shape[0]
    assert m % tm == 0
    return pl.pallas_call(
        _outproj_kernel,
        grid=(m // tm,),
        in_specs=[pl.BlockSpec((tm, D_MODEL), lambda i: (i, 0)),
                  pl.BlockSpec((tm, D_HG), lambda i: (i, 0)),
                  pl.BlockSpec((tm, D_NSA), lambda i: (i, 0)),
                  pl.BlockSpec((D_MODEL, D_MODEL), lambda i: (0, 0)),
                  pl.BlockSpec((1, D_MODEL), lambda i: (0, 0)),
                  pl.BlockSpec((1, D_MODEL), lambda i: (0, 0))],
        out_specs=pl.BlockSpec((tm, D_MODEL), lambda i: (i, 0)),
        out_shape=jax.ShapeDtypeStruct((m, D_MODEL), F32),
        compiler_params=_cparams(("arbitrary",)),
        name="outproj_ln",
    )(x2d, o_hg, o_nsa, w_out_b, ln_g.reshape(1, D_MODEL), ln_b.reshape(1, D_MODEL))


def _hgrn_gmat():
    c = HG_CHUNK
    tri = np.tril(np.ones((c, c), np.float32))
    mats = [tri]
    for half in HG_HALVES:
        t = np.arange(c)
        mid = (t // (2 * half)) * (2 * half) + half
        mats.append(tri - tri[mid - 1])
    return jnp.asarray(np.concatenate(mats, axis=0), BF16)


def _hgrn_lower_bound(lb_ref):
    raw = lb_ref[...]
    mx = jnp.max(raw, axis=0, keepdims=True)
    e = jnp.exp(raw - mx)
    return e[0:1, :] / jnp.sum(e, axis=0, keepdims=True)


def _hgrn_kernel(lb_ref, ng_ref, gm_ref, q_ref, f_ref, i_ref, g_ref, o_ref, s_ref, st_ref):
    c = HG_CHUNK
    t_len = q_ref.shape[0]
    lb2 = _hgrn_lower_bound(lb_ref)
    st_ref[...] = jnp.zeros_like(st_ref)

    row = lax.broadcasted_iota(jnp.int32, (c, HG_DK), 0)
    sub3 = lax.broadcasted_iota(jnp.int32, (c // SUBLANES, SUBLANES, HG_DK), 1)
    r2 = lax.broadcasted_iota(jnp.int32, (c, c), 0)
    c2 = lax.broadcasted_iota(jnp.int32, (c, c), 1)

    n_chunks = t_len // c

    def gates(ci):
        fr2 = f_ref[pl.ds(pl.multiple_of(ci * c, c), c), :]
        logf2 = jnp.log(lb2 + (1.0 - lb2) * jax.nn.sigmoid(fr2))
        return (1.0 - lb2) * jax.nn.sigmoid(-fr2), _dot3(gm_ref[...], logf2)

    def chunk(ci, carry):
        k2, dall2 = carry
        nxt = gates(jnp.minimum(ci + 1, n_chunks - 1))
        sl = pl.ds(pl.multiple_of(ci * c, c), c)
        for hh in range(HG_HPS):
            hs = slice(hh * HG_DK, (hh + 1) * HG_DK)
            one_head(sl, hh, q_ref[sl, hs], k2[:, hs], i_ref[sl, hs], dall2[:, hs], g_ref[sl, hs],
                     ng_ref[:, hs])
        return nxt

    def one_head(sl, hh, q, k, v, dall, gate, ng):
        a = dall[0:c]

        q3 = q.reshape(c // SUBLANES, SUBLANES, HG_DK)
        k3 = k.reshape(c // SUBLANES, SUBLANES, HG_DK)
        v3 = v.reshape(c // SUBLANES, SUBLANES, HG_DV)
        a3 = a.reshape(c // SUBLANES, SUBLANES, HG_DK)
        od = jnp.sum(q3 * k3, axis=-1, keepdims=True) * v3
        for d in range(1, HG_DIAG):
            ks = pltpu.roll(k3, d, 1)
            as_ = pltpu.roll(a3, d, 1)
            vs = pltpu.roll(v3, d, 1)
            ok = (sub3 % HG_DIAG) >= d
            w = jnp.where(ok, q3 * ks * jnp.exp(jnp.minimum(a3 - as_, 0.0)), 0.0)
            od = od + jnp.sum(w, axis=-1, keepdims=True) * vs
        od = od.reshape(c, HG_DV)

        att = jnp.zeros((c, c), F32)
        for lvl, half in enumerate(HG_HALVES, start=1):
            dl = dall[lvl * c:(lvl + 1) * c]
            e = jnp.exp(-jnp.abs(dl))
            upper = (row % (2 * half)) >= half
            qt = jnp.where(upper, q * e, 0.0).astype(BF16)
            kt = jnp.where(upper, 0.0, k * e).astype(BF16)
            al = _dot_nt(qt, kt)
            if 2 * half < c:
                al = jnp.where((r2 // (2 * half)) == (c2 // (2 * half)), al, 0.0)
            att = att + al
        vb = v.astype(BF16)
        o = od + _dot(att.astype(BF16), vb)

        st = st_ref[hh]
        o = o + _dot_nt((q * jnp.exp(a)).astype(BF16), st.astype(BF16))
        alast = a[c - 1:c, :]
        kd = (k * jnp.exp(alast - a)).astype(BF16)
        st_ref[hh] = st * jnp.exp(alast) + _dot(v.T.astype(BF16), kd)

        o = o * lax.rsqrt(jnp.mean(o * o, axis=-1, keepdims=True) + RMS_EPS)
        o_ref[sl, hh * HG_DV:(hh + 1) * HG_DV] = (o * ng * _silu(gate)).astype(o_ref.dtype)

    lax.fori_loop(0, n_chunks, chunk, gates(0))
    for hh in range(HG_HPS):
        s_ref[hh] = st_ref[hh].T


def _hgrn_prompt(hg, hg_lb, norm_g, b, t):
    assert t % HG_CHUNK == 0
    hg3 = hg.reshape(b, t, 4 * D_HG)
    gm = _hgrn_gmat()

    n_hb = HG_HEADS // HG_HPS
    wide = HG_HPS * HG_DK

    def col(k):
        return pl.BlockSpec((None, t, wide), lambda bi, h: (bi, 0, k * n_hb + h))

    return pl.pallas_call(
        _hgrn_kernel,
        grid=(b, n_hb),
        in_specs=[pl.BlockSpec((2, wide), lambda bi, h: (0, h)),
                  pl.BlockSpec((1, wide), lambda bi, h: (0, h)),
                  pl.BlockSpec(gm.shape, lambda bi, h: (0, 0)),
                  col(0), col(1), col(2), col(3)],
        out_specs=[pl.BlockSpec((None, t, wide), lambda bi, h: (bi, 0, h)),
                   pl.BlockSpec((None, HG_HPS, HG_DK, HG_DV), lambda bi, h: (bi, h, 0, 0))],
        out_shape=[jax.ShapeDtypeStruct((b, t, D_HG), BF16),
                   jax.ShapeDtypeStruct((b, HG_HEADS, HG_DK, HG_DV), F32)],
        scratch_shapes=[pltpu.VMEM((HG_HPS, HG_DV, HG_DK), F32)],
        compiler_params=_cparams(("arbitrary", "arbitrary")),
        name="hgrn_prompt",
    )(hg_lb, norm_g.reshape(1, D_HG), gm, hg3, hg3, hg3, hg3)


def _hgrn_step_kernel(lb_ref, ng_ref, x_ref, s_ref, o_ref, so_ref):
    raw = lb_ref[...]
    mx = jnp.max(raw, axis=0)
    e0 = jnp.exp(raw[0] - mx)
    lb_all = e0 / (e0 + jnp.exp(raw[1] - mx))
    x = x_ref[...]
    eye = (lax.broadcasted_iota(jnp.int32, (HG_DK, HG_DK), 0)
           == lax.broadcasted_iota(jnp.int32, (HG_DK, HG_DK), 1))

    def column(rowvec):
        return jnp.sum(jnp.where(eye, rowvec, 0.0), axis=1, keepdims=True)

    for h in range(HG_HEADS):
        lb = lb_all[h:h + 1]
        q = x[h:h + 1]
        fr = x[HG_HEADS + h:HG_HEADS + h + 1]
        v = x[2 * HG_HEADS + h:2 * HG_HEADS + h + 1]
        gate = x[3 * HG_HEADS + h:3 * HG_HEADS + h + 1]
        f = lb + (1.0 - lb) * jax.nn.sigmoid(fr)
        k = (1.0 - lb) * jax.nn.sigmoid(-fr)
        s_new = s_ref[h] * column(f) + column(k) * v
        so_ref[h] = s_new
        o = jnp.sum(column(q) * s_new, axis=0, keepdims=True)
        o = o * lax.rsqrt(jnp.mean(o * o, axis=-1, keepdims=True) + RMS_EPS)
        o_ref[h:h + 1, :] = o * ng_ref[h:h + 1, :] * _silu(gate)


def _hgrn_sample(hg_s, state, hg_lb, norm_g):
    b = hg_s.shape[0]
    x3 = hg_s.reshape(b, 4 * HG_HEADS, HG_DK)
    return pl.pallas_call(
        _hgrn_step_kernel,
        grid=(b,),
        in_specs=[pl.BlockSpec((2, HG_HEADS, HG_DK), lambda i: (0, 0, 0)),
                  pl.BlockSpec((HG_HEADS, HG_DV), lambda i: (0, 0)),
                  pl.BlockSpec((None, 4 * HG_HEADS, HG_DK), lambda i: (i, 0, 0)),
                  pl.BlockSpec((None, HG_HEADS, HG_DK, HG_DV), lambda i: (i, 0, 0, 0))],
        out_specs=[pl.BlockSpec((None, HG_HEADS, HG_DV), lambda i: (i, 0, 0)),
                   pl.BlockSpec((None, HG_HEADS, HG_DK, HG_DV), lambda i: (i, 0, 0, 0))],
        out_shape=[jax.ShapeDtypeStruct((b, HG_HEADS, HG_DV), F32),
                   jax.ShapeDtypeStruct((b, HG_HEADS, HG_DK, HG_DV), F32)],
        compiler_params=_cparams(("arbitrary",)),
        name="hgrn_sample",
    )(hg_lb.reshape(2, HG_HEADS, HG_DK), norm_g.reshape(HG_HEADS, HG_DV), x3, state)


def _cmp_weights(cmp_w1, cmp_w2):
    w1 = cmp_w1.reshape(2, CMP_R, CMP_D, NSA_DH, CMP_HID)
    w1 = jnp.transpose(w1, (0, 2, 3, 1, 4)).reshape(2, CMP_D, NSA_DH, CMP_R * CMP_HID)
    z1 = jnp.zeros_like(w1)
    w1p = jnp.stack([jnp.concatenate([w1, z1], axis=2), jnp.concatenate([z1, w1], axis=2)], axis=1)
    z2 = jnp.zeros_like(cmp_w2)
    w2p = jnp.stack([jnp.concatenate([cmp_w2, z2], axis=2), jnp.concatenate([z2, cmp_w2], axis=2)], axis=1)
    return w1p.astype(BF16), w2p.astype(BF16)


def _pe_term(pe_ref, w1f_ref):
    pe = jnp.broadcast_to(pe_ref[...], (SUBLANES, CMP_L * NSA_DH))
    w = w1f_ref[...]
    wh, wm, wl = _split3(w)
    ph, pm, pl_ = _split3(pe)
    out = (_dot(ph, wh) + _dot(ph, wm) + _dot(pm, wh)
           + _dot(ph, wl) + _dot(pm, wm) + _dot(pl_, wh))
    return out[0:1, :]


def _cmp_prompt_kernel(x_ref, w1p_ref, w2p_ref, pe_ref, w1f_ref, o_ref):
    n_sub = x_ref.shape[0] // CMP_D
    pe_t = _pe_term(pe_ref, w1f_ref)
    out = jnp.zeros((n_sub, LANES), F32)
    for par in range(2):
        acc = jnp.zeros((n_sub, CMP_R * CMP_HID), F32)
        for j in range(CMP_D):
            xj = x_ref[pl.ds(j, n_sub, stride=CMP_D), :].astype(BF16)
            acc = acc + _dot(xj, w1p_ref[par, j])
        nxt = pltpu.roll(acc[:, CMP_HID:], n_sub - 1, 0)
        hid = pe_t + acc[:, :CMP_HID] + nxt
        out = out + _dot(_silu(hid).astype(BF16), w2p_ref[par])
    o_ref[...] = out.astype(o_ref.dtype)


def _cmp_prompt(kv, w1p, w2p, cmp_w1, cmp_pe, b, t):
    assert t % CMP_D == 0 and (t // CMP_D) % SUBLANES == 0
    n_sub = t // CMP_D
    kv3 = kv.reshape(b, t, kv.shape[-1])
    pe = cmp_pe.reshape(2, 1, CMP_L * NSA_DH)
    w1f = cmp_w1.reshape(2, CMP_L * NSA_DH, CMP_HID)
    return pl.pallas_call(
        _cmp_prompt_kernel,
        grid=(b, 2, 2),
        in_specs=[pl.BlockSpec((None, t, LANES), lambda bi, w, p: (bi, 0, 2 * w + p)),
                  pl.BlockSpec((None, 2, CMP_D, LANES, CMP_R * CMP_HID), lambda bi, w, p: (w, 0, 0, 0, 0)),
                  pl.BlockSpec((None, 2, CMP_HID, LANES), lambda bi, w, p: (w, 0, 0, 0)),
                  pl.BlockSpec((None, 1, CMP_L * NSA_DH), lambda bi, w, p: (w, 0, 0)),
                  pl.BlockSpec((None, CMP_L * NSA_DH, CMP_HID), lambda bi, w, p: (w, 0, 0))],
        out_specs=pl.BlockSpec((None, None, n_sub, LANES), lambda bi, w, p: (bi, w, 0, p)),
        out_shape=jax.ShapeDtypeStruct((b, 2, n_sub, KV_DIM), BF16),
        compiler_params=_cparams(("arbitrary", "arbitrary", "arbitrary")),
        name="cmp_prompt",
    )(kv3, w1p, w2p, pe, w1f)


TQ = 256
SLC_TK = 512


def _overlap_t(n_slc, n_cmp_pad):
    ci = np.arange(n_cmp_pad)[None, :] * CMP_D
    sj = np.arange(n_slc)[:, None] * SLC_L
    return ((ci < sj + SLC_L) & (ci + CMP_L > sj)).astype(np.float32)


def _gate_expand(pair):
    m = np.zeros((3, LANES, 2 * NSA_GROUP * NSA_DH), np.float32)
    for br in range(3):
        for hp in range(2):
            for g in range(NSA_GROUP):
                src = br * NSA_HEADS + (2 * pair + hp) * NSA_GROUP + g
                c0 = (hp * NSA_GROUP + g) * NSA_DH
                m[br, src, c0:c0 + NSA_DH] = 1.0
    return m


def _place_pair(x_even, x_odd, hp, lane):
    if hp == 0:
        return jnp.where(lane < NSA_DH, x_even, pltpu.roll(x_odd, NSA_DH, 1))
    return jnp.where(lane < NSA_DH, pltpu.roll(x_even, NSA_DH, 1), x_odd)


def _nsa_prompt_kernel(q_ref, sk_ref, sv_ref, wk_ref, wv_ref, kc_ref, vc_ref, gate_ref, gates_ref,
                       ovt_ref, gex_ref, o_ref, ka_ref, va_ref, wka_ref, wva_ref):
    i = pl.program_id(2)
    tq = q_ref.shape[0]
    t_len = sk_ref.shape[0]
    n_slc = ovt_ref.shape[0]
    n_cmp_pad = kc_ref.shape[0]
    rows = 2 * NSA_GROUP * tq
    n_sel = min(N_SEL, n_slc)

    @pl.when(i == 0)
    def _():
        ones = jnp.ones((t_len, LANES), BF16)
        jl = lax.broadcasted_iota(jnp.int32, (t_len, LANES), 1)
        ka_ref[:, 0:LANES] = sk_ref[...]
        kb = lax.broadcasted_iota(jnp.int32, (t_len, LANES), 0) // SLC_L
        ka_ref[:, LANES:2 * LANES] = jnp.where(kb == jl, 1.0, 0.0).astype(BF16)
        va_ref[0] = jnp.where(jl < NSA_DH, sv_ref[...], ones)
        va_ref[1] = jnp.where(jl < NSA_DH, ones, sv_ref[...])
        jp = lax.broadcasted_iota(jnp.int32, (WINDOW, 2 * LANES), 1)
        wka_ref[0:WINDOW, :] = jnp.where(jp == LANES, NEG, 0.0).astype(BF16)
        wka_ref[WINDOW:WINDOW + t_len, 0:LANES] = wk_ref[...]
        wka_ref[WINDOW:WINDOW + t_len, LANES:2 * LANES] = jnp.zeros((t_len, LANES), BF16)
        for hp in range(2):
            wva_ref[hp, 0:WINDOW, :] = jnp.zeros((WINDOW, LANES), BF16)
        wva_ref[0, WINDOW:WINDOW + t_len, :] = jnp.where(jl < NSA_DH, wv_ref[...], ones)
        wva_ref[1, WINDOW:WINDOW + t_len, :] = jnp.where(jl < NSA_DH, ones, wv_ref[...])

    lane = lax.broadcasted_iota(jnp.int32, (tq, LANES), 1)
    qpos_r = i * tq + lax.broadcasted_iota(jnp.int32, (rows, 1), 0) % tq
    sig_gates = jax.nn.sigmoid(gates_ref[...])

    qs = []
    for hp in range(2):
        keep = (lane >= NSA_DH) if hp == 1 else (lane < NSA_DH)
        for g in range(NSA_GROUP):
            blk = hp * 2 + g // 2
            x = q_ref[:, blk * LANES:(blk + 1) * LANES]
            if g % 2 != hp:
                x = pltpu.roll(x, NSA_DH, 1)
            qs.append(jnp.where(keep, x, jnp.zeros_like(x)))
    qs = jnp.concatenate(qs, axis=0) * jnp.asarray(NSA_DH ** -0.5, BF16)

    s = _dot_nt(qs, kc_ref[...])
    end = lax.broadcasted_iota(jnp.int32, (rows, n_cmp_pad), 1) * CMP_D + (CMP_L - 1)
    msk = end <= qpos_r
    s = jnp.where(msk, s, NEG)
    m = jnp.max(s, axis=-1, keepdims=True)
    p = jnp.where(msk, jnp.exp(s - m), 0.0)
    den = jnp.sum(p, axis=-1, keepdims=True)
    p = p / jnp.where(den > 0, den, 1.0)
    o_c = _dot(p.astype(BF16), vc_ref[...])

    jb = lax.broadcasted_iota(jnp.int32, (n_slc, tq), 0)
    qpos_l = i * tq + lax.broadcasted_iota(jnp.int32, (n_slc, tq), 1)
    cur = qpos_l // SLC_L
    forced = (jb == 0) | (jb == cur) | (jb == cur - 1)
    valid = jb * SLC_L <= qpos_l
    selfeat = []
    for hp in range(2):
        r0 = hp * NSA_GROUP * tq
        psum = p[r0:r0 + tq] + p[r0 + tq:r0 + 2 * tq] + p[r0 + 2 * tq:r0 + 3 * tq] + p[r0 + 3 * tq:r0 + 4 * tq]
        imp_t = _dot3_nt(ovt_ref[...], psum)
        score = jnp.where(forced, FORCED_SCORE, jnp.where(valid, imp_t, -1.0))
        cnt = jnp.zeros((n_slc, tq), jnp.int32)
        for jp in range(n_slc):
            r = score[jp:jp + 1, :]
            beats = (r > score) | ((r == score) & (jb > jp))
            cnt = cnt + beats.astype(jnp.int32)
        self_t = jnp.where(cnt < n_sel, 0.0, NEG)
        if n_slc < LANES:
            self_t = jnp.concatenate([self_t, jnp.zeros((LANES - n_slc, tq), F32)], axis=0)
        selfeat += [self_t.T.astype(BF16)] * NSA_GROUP
    qa = jnp.concatenate([qs, jnp.concatenate(selfeat, axis=0)], axis=1)

    tk = min(SLC_TK, t_len)

    hrows = rows // 2
    qa_h = (qa[:hrows], qa[hrows:])
    qpos_h = qpos_r[:hrows]

    def slc_tile(kt, carry, causal):
        ks = pl.ds(pl.multiple_of(kt * tk, tk), tk)
        out = []
        for hp in range(2):
            m_i, acc = carry[hp]
            s2 = _dot_nt(qa_h[hp], ka_ref[ks, :])
            if causal:
                kpos = kt * tk + lax.broadcasted_iota(jnp.int32, (hrows, tk), 1)
                s2 = jnp.where(kpos <= qpos_h, s2, NEG)
            m_n = jnp.maximum(m_i, jnp.max(s2, axis=-1, keepdims=True))
            al = jnp.exp(m_i - m_n)
            p2 = jnp.exp((s2 - m_n).astype(BF16))
            out.append((m_n, al * acc + _dot(p2, va_ref[hp, ks, :])))
        return tuple(out)

    lane_h = lax.broadcasted_iota(jnp.int32, (hrows, LANES), 1)

    def normalise(acc, hp):
        head = (lane_h >= NSA_DH) if hp == 1 else (lane_h < NSA_DH)
        return acc / jnp.where(head, pltpu.roll(acc, NSA_DH, 1), 1.0)

    n_full = (i * tq) // tk
    init = (jnp.full((hrows, 1), NEG, F32), jnp.zeros((hrows, LANES), F32))
    carry = lax.fori_loop(0, n_full, functools.partial(slc_tile, causal=False), (init, init))
    carry = slc_tile(n_full, carry, True)
    o_s = jnp.concatenate([normalise(carry[0][1], 0), normalise(carry[1][1], 1)], axis=0)

    lw = WINDOW + tq
    ws = pl.ds(pl.multiple_of(i * tq, tq), lw)
    qw = jnp.concatenate([qs, jnp.where(lax.broadcasted_iota(jnp.int32, (rows, LANES), 1) == 0, 1.0, 0.0
                                        ).astype(BF16)], axis=1)
    cr = (lax.broadcasted_iota(jnp.int32, (hrows, tq), 1)
          - lax.broadcasted_iota(jnp.int32, (hrows, tq), 0) % tq)
    o_w = []
    for hp in range(2):
        s3 = _dot_nt(qw[hp * hrows:(hp + 1) * hrows], wka_ref[ws, :])
        s3 = jnp.concatenate([jnp.where(cr > 0, s3[:, :tq], NEG), s3[:, tq:lw - tq],
                              jnp.where(cr <= 0, s3[:, lw - tq:], NEG)], axis=1)
        m3 = jnp.max(s3, axis=-1, keepdims=True)
        p3 = jnp.exp((s3 - m3).astype(BF16))
        o_w.append(normalise(_dot(p3, wva_ref[hp, ws, :]), hp))
    o_w = jnp.concatenate(o_w, axis=0)

    sig_hi = sig_gates.astype(BF16)
    sig_lo = (sig_gates - sig_hi.astype(F32)).astype(BF16)
    out = None
    for br, ob in enumerate((o_c, o_s, o_w)):
        parts = []
        for hp in range(2):
            for gp in range(2):
                r0 = (hp * NSA_GROUP + 2 * gp) * tq
                parts.append(_place_pair(ob[r0:r0 + tq], ob[r0 + tq:r0 + 2 * tq], hp, lane))
        gexp = _dot(sig_hi, gex_ref[br]) + _dot(sig_lo, gex_ref[br])
        term = gexp * jnp.concatenate(parts, axis=1)
        out = term if out is None else out + term
    o_ref[...] = (out * _silu(gate_ref[...])).astype(o_ref.dtype)


def _nsa_prompt(q, kvb, kcv, g, b, t):
    assert t % TQ == 0 and t % SLC_L == 0 and (t <= SLC_TK or t % SLC_TK == 0)
    n_sub = t // CMP_D
    n_slc = t // SLC_L
    assert n_slc <= LANES
    q3 = q.reshape(b, t, D_NSA)
    kvb3 = kvb.reshape(b, t, 6 * KV_DIM)
    g3 = g.reshape(b, t, G_WIDTH)
    ovt = jnp.asarray(_overlap_t(n_slc, n_sub), BF16)
    gex = jnp.asarray(np.stack([_gate_expand(0), _gate_expand(1)]), BF16)
    pw = 2 * NSA_GROUP * NSA_DH

    def kvcol(which):
        return pl.BlockSpec((None, t, LANES), lambda bi, p, i: (bi, 0, 2 * which + p))

    return pl.pallas_call(
        _nsa_prompt_kernel,
        grid=(b, 2, t // TQ),
        in_specs=[pl.BlockSpec((None, TQ, pw), lambda bi, p, i: (bi, i, p)),
                  kvcol(2), kvcol(3), kvcol(4), kvcol(5),
                  pl.BlockSpec((None, None, n_sub, LANES), lambda bi, p, i: (bi, 0, 0, p)),
                  pl.BlockSpec((None, None, n_sub, LANES), lambda bi, p, i: (bi, 1, 0, p)),
                  pl.BlockSpec((None, TQ, pw), lambda bi, p, i: (bi, i, p)),
                  pl.BlockSpec((None, TQ, LANES), lambda bi, p, i: (bi, i, D_NSA // LANES)),
                  pl.BlockSpec(ovt.shape, lambda bi, p, i: (0, 0)),
                  pl.BlockSpec((None, 3, LANES, pw), lambda bi, p, i: (p, 0, 0, 0))],
        out_specs=pl.BlockSpec((None, TQ, pw), lambda bi, p, i: (bi, i, p)),
        out_shape=jax.ShapeDtypeStruct((b, t, D_NSA), BF16),
        scratch_shapes=[pltpu.VMEM((t, 2 * LANES), BF16), pltpu.VMEM((2, t, LANES), BF16),
                        pltpu.VMEM((t + WINDOW, 2 * LANES), BF16), pltpu.VMEM((2, t + WINDOW, LANES), BF16)],
        compiler_params=_cparams(("arbitrary", "arbitrary", "arbitrary")),
        name="nsa_prompt",
    )(q3, kvb3, kvb3, kvb3, kvb3, kcv, kcv, g3, g3, ovt, gex)


CMP_PAGES = 32


def _cmp_bd_weights(cmp_w1, cmp_w2):
    w1 = cmp_w1.reshape(2, CMP_R, CMP_D, NSA_DH, CMP_HID)
    w1 = jnp.transpose(w1, (0, 2, 3, 1, 4)).reshape(2, CMP_D, NSA_DH, CMP_R * CMP_HID)
    eye2 = jnp.eye(2, dtype=w1.dtype)
    bd = jnp.einsum('wjdn,hg->wjhdgn', w1, eye2).reshape(2, CMP_D, LANES, 2 * CMP_R * CMP_HID)
    eye4 = jnp.eye(NSA_KV_HEADS, dtype=w1.dtype)
    w2q = jnp.einsum('wed,hg->whegd', cmp_w2, eye4).reshape(2, NSA_KV_HEADS, CMP_HID, KV_DIM)
    return bd.astype(BF16), w2q.astype(BF16)


def _cmp_sample_kernel(pt_ref, cache_ref, perm_ref, bd_ref, w2q_ref, pe_ref, w1f_ref, o_ref, tbuf_ref, buf_ref,
                       acc_ref, sem_ref, *, pages, n_b, n_c):
    w = pl.program_id(0)
    b = pl.program_id(1)
    c = pl.program_id(2)
    step = (w * n_b + b) * n_c + c
    total = 2 * n_b * n_c
    slot = step % 2
    rows = pages * PAGE_SIZE // CMP_D

    def copies(w_, b_, c_, slot_):
        return [pltpu.make_async_copy(
            cache_ref.at[pt_ref[b_, c_ * pages + p], pl.ds(pl.multiple_of(w_ * KV_DIM, KV_DIM), KV_DIM), :],
            tbuf_ref.at[slot_, p], sem_ref.at[slot_]) for p in range(pages)]

    @pl.when(step == 0)
    def _():
        for cp in copies(w, b, c, slot):
            cp.start()

    @pl.when(step + 1 < total)
    def _():
        nxt = step + 1
        c_n = nxt % n_c
        b_n = (nxt // n_c) % n_b
        w_n = nxt // (n_c * n_b)
        for cp in copies(w_n, b_n, c_n, 1 - slot):
            cp.start()

    for cp in copies(w, b, c, slot):
        cp.wait()

    def to_rows(p, carry):
        xp = _dot_nt(perm_ref[...], tbuf_ref[slot, p].astype(BF16))
        for half in range(2):
            buf_ref[half, p] = xp[:, half * LANES:(half + 1) * LANES].reshape(CMP_D, n_pp, LANES)
        return carry

    n_pp = PAGE_SIZE // CMP_D
    lax.fori_loop(0, pages, to_rows, 0, unroll=8 if pages % 8 == 0 else 1)

    pw = 2 * CMP_R * CMP_HID
    for half in range(2):
        acc = jnp.zeros((rows, pw), F32)
        for jp in range(CMP_D // 2):
            xa = buf_ref[half, :, 2 * jp].reshape(rows, LANES)
            xb = buf_ref[half, :, 2 * jp + 1].reshape(rows, LANES)
            acc = acc + _dot(jnp.concatenate([xa, xb], axis=1).astype(BF16), bd_ref[jp])
        acc_ref[pl.ds(pl.multiple_of(c * rows, rows), rows), half * pw:(half + 1) * pw] = acc

    @pl.when(c == n_c - 1)
    def _():
        n_sub = acc_ref.shape[0]
        pe_t = _pe_term(pe_ref, w1f_ref)
        out = jnp.zeros((n_sub, KV_DIM), F32)
        for h in range(NSA_KV_HEADS):
            c0 = h * CMP_R * CMP_HID
            nxt = pltpu.roll(acc_ref[:, c0 + CMP_HID:c0 + 2 * CMP_HID], n_sub - 1, 0)
            hid = pe_t + acc_ref[:, c0:c0 + CMP_HID] + nxt
            out = out + _dot(_silu(hid).astype(BF16), w2q_ref[h])
        o_ref[...] = out.astype(o_ref.dtype)


def _cmp_sample(cache_t, page_table, bd, w2q, cmp_w1, cmp_pe):
    n_b, n_pages = page_table.shape
    pages = min(CMP_PAGES, n_pages)
    assert n_pages % pages == 0
    n_c = n_pages // pages
    n_sub = n_pages * PAGE_SIZE // CMP_D
    pe = cmp_pe.reshape(2, 1, CMP_L * NSA_DH)
    w1f = cmp_w1.reshape(2, CMP_L * NSA_DH, CMP_HID)
    nbd = NSA_KV_HEADS * CMP_R * CMP_HID
    n_pp = PAGE_SIZE // CMP_D
    perm = np.zeros((PAGE_SIZE, PAGE_SIZE), np.float32)
    jn = np.arange(PAGE_SIZE)
    perm[jn, (jn % n_pp) * CMP_D + jn // n_pp] = 1.0
    perm = jnp.asarray(perm, BF16)
    bd2 = bd.reshape(2, CMP_D // 2, 2 * LANES, nbd // 2)
    kern = functools.partial(_cmp_sample_kernel, pages=pages, n_b=n_b, n_c=n_c)
    return pl.pallas_call(
        kern,
        grid_spec=pltpu.PrefetchScalarGridSpec(
            num_scalar_prefetch=1,
            grid=(2, n_b, n_c),
            in_specs=[pl.BlockSpec(memory_space=pl.ANY),
                      pl.BlockSpec((PAGE_SIZE, PAGE_SIZE), lambda w, b, c, pt: (0, 0)),
                      pl.BlockSpec((None, CMP_D // 2, 2 * LANES, nbd // 2), lambda w, b, c, pt: (w, 0, 0, 0)),
                      pl.BlockSpec((None, NSA_KV_HEADS, CMP_HID, KV_DIM), lambda w, b, c, pt: (w, 0, 0, 0)),
                      pl.BlockSpec((None, 1, CMP_L * NSA_DH), lambda w, b, c, pt: (w, 0, 0)),
                      pl.BlockSpec((None, CMP_L * NSA_DH, CMP_HID), lambda w, b, c, pt: (w, 0, 0))],
            out_specs=pl.BlockSpec((None, None, n_sub, KV_DIM), lambda w, b, c, pt: (b, w, 0, 0)),
            scratch_shapes=[pltpu.VMEM((2, pages, KV_DIM, PAGE_SIZE), F32),
                            pltpu.VMEM((2, pages, CMP_D, n_pp, LANES), F32),
                            pltpu.VMEM((n_sub, nbd), F32),
                            pltpu.SemaphoreType.DMA((2,))]),
        out_shape=jax.ShapeDtypeStruct((n_b, 2, n_sub, KV_DIM), BF16),
        compiler_params=_cparams(("arbitrary", "arbitrary", "arbitrary")),
        name="cmp_sample",
    )(page_table, cache_t, perm, bd2, w2q, pe, w1f)


def _nsa_sample_a_kernel(qh_ref, kc_ref, vc_ref, ov_ref, oc_ref, sel_ref, *, past_len, n_slc):
    n_sub = kc_ref.shape[0]
    n_slc_pad = ov_ref.shape[1]
    n_sel = min(N_SEL, n_slc)
    qpos = past_len
    rowi = lax.broadcasted_iota(jnp.int32, (SUBLANES, n_sub), 0)
    rowq = lax.broadcasted_iota(jnp.int32, (SROWS, n_sub), 0)
    ncol = lax.broadcasted_iota(jnp.int32, (SROWS, n_sub), 1)
    msk = (ncol * CMP_D + (CMP_L - 1) <= qpos) & (ncol < n_sub - 1)
    psum = jnp.zeros((SUBLANES, n_sub), F32)
    for h in range(NSA_KV_HEADS):
        pr = h // 2
        qh = (qh_ref[h] * NSA_DH ** -0.5).astype(BF16)
        s = _dot_nt(qh, kc_ref[:, pr * LANES:(pr + 1) * LANES])
        s = jnp.where(msk, s, NEG)
        m = jnp.max(s, axis=-1, keepdims=True)
        p = jnp.where(msk, jnp.exp(s - m), 0.0)
        den = jnp.sum(p, axis=-1, keepdims=True)
        p = p / jnp.where(den > 0, den, 1.0)
        oc_ref[h] = _dot(p.astype(BF16), vc_ref[:, pr * LANES:(pr + 1) * LANES])
        ph = jnp.sum(jnp.where(rowq < NSA_GROUP, p, 0.0), axis=0, keepdims=True)
        psum = jnp.where(rowi == h, ph, psum)

    hi, mid, lo = _split3(psum)
    imp = _dot(hi, ov_ref[...]) + _dot(mid, ov_ref[...]) + _dot(lo, ov_ref[...])
    jb = lax.broadcasted_iota(jnp.int32, (SUBLANES, n_slc_pad), 1)
    cur = qpos // SLC_L
    forced = (jb == 0) | (jb == cur) | (jb == cur - 1)
    valid = jb * SLC_L <= qpos
    score = jnp.where(forced, FORCED_SCORE, jnp.where(valid, imp, -1.0))
    score = jnp.where(jb < n_slc, score, -2.0)
    cnt = jnp.zeros((SUBLANES, n_slc_pad), jnp.int32)
    for jp in range(n_slc):
        r = score[:, jp:jp + 1]
        beats = (r > score) | ((r == score) & (jb > jp))
        cnt = cnt + beats.astype(jnp.int32)
    cnt = jnp.where(jb < n_slc, cnt, n_slc_pad)
    lane = lax.broadcasted_iota(jnp.int32, (SUBLANES, LANES), 1)
    idx = jnp.zeros((SUBLANES, LANES), jnp.int32)
    for slot in range(n_sel):
        pick = jnp.sum(jnp.where(cnt == slot, jb, 0), axis=1, keepdims=True)
        idx = jnp.where(lane == slot, pick, idx)
    sel_ref[...] = idx


def _nsa_sample_a(qh, kcv, past_len):
    n_b = qh.shape[0]
    n_sub = kcv.shape[2]
    n_slc = -(-(past_len + 1) // SLC_L)
    n_slc_pad = -(-n_slc // LANES) * LANES
    ov = np.zeros((n_sub, n_slc_pad), np.float32)
    ov[:, :n_slc] = _overlap_t(n_slc, n_sub).T
    ov = jnp.asarray(ov, BF16)
    kern = functools.partial(_nsa_sample_a_kernel, past_len=past_len, n_slc=n_slc)
    return pl.pallas_call(
        kern,
        grid=(n_b,),
        in_specs=[pl.BlockSpec((None, NSA_KV_HEADS, SROWS, LANES), lambda b: (b, 0, 0, 0)),
                  pl.BlockSpec((None, None, n_sub, KV_DIM), lambda b: (b, 0, 0, 0)),
                  pl.BlockSpec((None, None, n_sub, KV_DIM), lambda b: (b, 1, 0, 0)),
                  pl.BlockSpec(ov.shape, lambda b: (0, 0))],
        out_specs=[pl.BlockSpec((None, NSA_KV_HEADS, SROWS, LANES), lambda b: (b, 0, 0, 0)),
                   pl.BlockSpec((None, SUBLANES, LANES), lambda b: (b, 0, 0))],
        out_shape=[jax.ShapeDtypeStruct((n_b, NSA_KV_HEADS, SROWS, LANES), F32),
                   jax.ShapeDtypeStruct((n_b, SUBLANES, LANES), jnp.int32)],
        compiler_params=_cparams(("arbitrary",)),
        name="nsa_sample_a",
    )(qh, kcv, kcv, ov), n_slc


def _nsa_sample_b_kernel(pt_ref, sel_ref, qh_ref, cache_ref, new_ref, win_ref, oc_ref, gates_ref, gate_ref,
                         o_ref, kbuf_ref, vbuf_ref, sem_ref, *, past_len, n_slc, n_sel, n_b):
    b = pl.program_id(0)
    slot = b % 2
    n_blocks_cached = past_len // SLC_L
    per_page = PAGE_SIZE // SLC_L

    def copies(b_, slot_):
        out = []
        for h in range(NSA_KV_HEADS):
            for k in range(n_sel):
                j = jnp.minimum(sel_ref[b_, h, k], n_blocks_cached - 1)
                page = pt_ref[b_, j // per_page]
                for which, buf in ((2, kbuf_ref), (3, vbuf_ref)):
                    out.append(pltpu.make_async_copy(
                        cache_ref.at[page, pl.ds(which * KV_DIM + (h // 2) * LANES, LANES), :],
                        buf.at[slot_, h, :, pl.ds(k * PAGE_SIZE, PAGE_SIZE)],
                        sem_ref.at[slot_]))
        return out

    @pl.when(b == 0)
    def _():
        for cp in copies(b, slot):
            cp.start()

    @pl.when(b + 1 < n_b)
    def _():
        for cp in copies(b + 1, 1 - slot):
            cp.start()

    for cp in copies(b, slot):
        cp.wait()

    qpos = past_len
    nk = n_sel * PAGE_SIZE
    lane_k = lax.broadcasted_iota(jnp.int32, (1, nk), 1)
    n_buf = win_ref.shape[1]
    widx = lax.broadcasted_iota(jnp.int32, (1, n_buf), 1)
    rel = n_buf - widx
    wmask = (rel >= 0) & (rel < WINDOW) & (past_len - n_buf + widx >= 0)
    rowg = lax.broadcasted_iota(jnp.int32, (SROWS, LANES), 0)

    for h in range(NSA_KV_HEADS):
        pr = h // 2
        ls = slice(pr * LANES, (pr + 1) * LANES)
        qf = qh_ref[h] * NSA_DH ** -0.5
        qh = qf.astype(BF16)

        def new_key(row):
            return new_ref[row:row + 1, ls].astype(BF16).astype(F32)

        blk = jnp.zeros((1, nk), jnp.int32)
        n_cur = jnp.int32(0)
        for k in range(n_sel):
            sj = sel_ref[b, h, k]
            blk = blk + jnp.where(lane_k // PAGE_SIZE == k, sj, 0)
            n_cur = n_cur + (sj == qpos // SLC_L).astype(jnp.int32)
        has_new = (jnp.zeros((1, 1), jnp.int32) + n_cur) > 0
        kvalid = ((lane_k % PAGE_SIZE) // SLC_L == blk % per_page) & (blk < n_blocks_cached)
        s = _dot(qh, kbuf_ref[slot, h].astype(BF16))
        s = jnp.where(kvalid, s, NEG)
        s_new = jnp.sum(qf * new_key(2), axis=-1, keepdims=True)
        s_new = jnp.where(has_new, s_new, NEG)
        m = jnp.maximum(jnp.max(s, axis=-1, keepdims=True), s_new)
        p = jnp.where(kvalid, jnp.exp(s - m), 0.0)
        p_new = jnp.where(has_new, jnp.exp(s_new - m), 0.0)
        den = jnp.sum(p, axis=-1, keepdims=True) + p_new
        den = jnp.where(den > 0, den, 1.0)
        o_s = (_dot_nt(p.astype(BF16), vbuf_ref[slot, h].astype(BF16)) + p_new * new_key(3)) / den

        s = _dot(qh, win_ref[pr * LANES:(pr + 1) * LANES, :].astype(BF16))
        s = jnp.where(wmask, s, NEG)
        s_new = jnp.sum(qf * new_key(4), axis=-1, keepdims=True)
        m = jnp.maximum(jnp.max(s, axis=-1, keepdims=True), s_new)
        p = jnp.where(wmask, jnp.exp(s - m), 0.0)
        p_new = jnp.exp(s_new - m)
        den = jnp.sum(p, axis=-1, keepdims=True) + p_new
        wv = win_ref[KV_DIM + pr * LANES:KV_DIM + (pr + 1) * LANES, :].astype(BF16)
        o_w = (_dot_nt(p.astype(BF16), wv) + p_new * new_key(5)) / den

        g = jax.nn.sigmoid(gates_ref[:, h])
        o = g[0] * oc_ref[h] + g[1] * o_s + g[2] * o_w
        o_ref[h] = jnp.where(rowg < NSA_GROUP, o * _silu(gate_ref[h]), 0.0)


def _nsa_sample_b(page_table, sel, qh, cache_t, kv_new, win_t, o_c, gates_l, gate_l, past_len, n_slc):
    n_b = qh.shape[0]
    n_sel = sel.shape[2]
    n_buf = win_t.shape[2]
    kern = functools.partial(_nsa_sample_b_kernel, past_len=past_len, n_slc=n_slc, n_sel=n_sel, n_b=n_b)
    hb = (None, NSA_KV_HEADS, SROWS, LANES)
    return pl.pallas_call(
        kern,
        grid_spec=pltpu.PrefetchScalarGridSpec(
            num_scalar_prefetch=2,
            grid=(n_b,),
            in_specs=[pl.BlockSpec(hb, lambda b, pt, sl: (b, 0, 0, 0)),
                      pl.BlockSpec(memory_space=pl.ANY),
                      pl.BlockSpec((None, 6, KV_DIM), lambda b, pt, sl: (b, 0, 0)),
                      pl.BlockSpec((None, 2 * KV_DIM, n_buf), lambda b, pt, sl: (b, 0, 0)),
                      pl.BlockSpec(hb, lambda b, pt, sl: (b, 0, 0, 0)),
                      pl.BlockSpec((None, 3, NSA_KV_HEADS, SROWS, LANES), lambda b, pt, sl: (b, 0, 0, 0, 0)),
                      pl.BlockSpec(hb, lambda b, pt, sl: (b, 0, 0, 0))],
            out_specs=pl.BlockSpec(hb, lambda b, pt, sl: (b, 0, 0, 0)),
            scratch_shapes=[pltpu.VMEM((2, NSA_KV_HEADS, LANES, n_sel * PAGE_SIZE), F32),
                            pltpu.VMEM((2, NSA_KV_HEADS, LANES, n_sel * PAGE_SIZE), F32),
                            pltpu.SemaphoreType.DMA((2,))]),
        out_shape=jax.ShapeDtypeStruct((n_b, NSA_KV_HEADS, SROWS, LANES), F32),
        compiler_params=_cparams(("arbitrary",)),
        name="nsa_sample_b",
    )(page_table, sel, qh, cache_t, kv_new, win_t, o_c, gates_l, gate_l)


def _head_layout(x):
    n_b = x.shape[0]
    x = x.astype(F32)
    z = jnp.zeros_like(x)
    even = jnp.concatenate([x, z], axis=-1)
    odd = jnp.concatenate([z, x], axis=-1)
    par = (jnp.arange(NSA_KV_HEADS) % 2).reshape(1, NSA_KV_HEADS, 1, 1)
    y = jnp.where(par == 0, even, odd)
    return jnp.concatenate([y, jnp.zeros((n_b, NSA_KV_HEADS, SROWS - NSA_GROUP, LANES), F32)], axis=2)


def _sample_path(x_sample, cache_kv, cache_win, state, page_table, w_p, w_out_b, hg_lb, norm_g,
                 cmp_w1, cmp_w2, cmp_pe, ln_g, ln_b):
    n_b, t_s, _ = x_sample.shape
    assert t_s == 1, "single-token decode"
    n_pool = cache_kv.shape[0]
    n_pages = page_table.shape[1]
    past_len = n_pages * PAGE_SIZE
    xs = x_sample.reshape(n_b, D_MODEL)
    hg, q, kv, _, g = _inproj(xs, w_p, n_b)
    o_hg, s_new = _hgrn_sample(hg, state, hg_lb, norm_g)

    cache_t = jnp.transpose(cache_kv, (0, 2, 3, 4, 1)).reshape(n_pool, 4 * KV_DIM, PAGE_SIZE)
    bd, w2q = _cmp_bd_weights(cmp_w1, cmp_w2)
    kcv = _cmp_sample(cache_t, page_table, bd, w2q, cmp_w1, cmp_pe)

    qh = _head_layout(q.reshape(n_b, NSA_KV_HEADS, NSA_GROUP, NSA_DH))
    (o_c, sel), n_slc = _nsa_sample_a(qh, kcv, past_len)
    n_sel = min(N_SEL, n_slc)
    sel = sel[:, :NSA_KV_HEADS, :n_sel]

    gates = g[:, D_NSA:D_NSA + 3 * NSA_HEADS].reshape(n_b, 3, NSA_KV_HEADS, NSA_GROUP, 1)
    gates_l = _head_layout_b(jnp.broadcast_to(gates, (n_b, 3, NSA_KV_HEADS, NSA_GROUP, NSA_DH)))
    gate_l = _head_layout(g[:, :D_NSA].reshape(n_b, NSA_KV_HEADS, NSA_GROUP, NSA_DH))
    n_buf = cache_win.shape[1]
    win_t = jnp.transpose(cache_win, (0, 2, 3, 4, 1)).reshape(n_b, 2 * KV_DIM, n_buf)
    o = _nsa_sample_b(page_table, sel, qh, cache_t, kv.reshape(n_b, 6, KV_DIM), win_t, o_c, gates_l, gate_l,
                      past_len, n_slc)
    o = o.reshape(n_b, NSA_KV_HEADS, SROWS, 2, NSA_DH)[:, :, :NSA_GROUP].sum(axis=3)
    o_nsa = o.reshape(n_b, D_NSA).astype(BF16)

    y = _outproj(xs, o_hg.reshape(n_b, D_HG).astype(BF16), o_nsa, w_out_b, ln_g, ln_b, n_b)
    kv6 = kv.reshape(n_b, 1, 6, NSA_KV_HEADS, NSA_DH)
    win_cat = jnp.concatenate([cache_win, kv6[:, :, 4:].astype(cache_win.dtype)], axis=1)
    return (y.reshape(n_b, 1, D_MODEL), kv6[None, :, :, :4].astype(cache_kv.dtype), win_cat[None, :, 1:],
            s_new[None].astype(state.dtype))


def _head_layout_b(x):
    n_b = x.shape[0]
    y = _head_layout(x.reshape(n_b * 3, NSA_KV_HEADS, NSA_GROUP, NSA_DH))
    return y.reshape(n_b, 3, NSA_KV_HEADS, SROWS, LANES)


def kernel(x_prompt, x_sample, cache_kv, cache_win, state_hgrn, page_table, w_in, hg_lb, hg_norm_g,
           cmp_w1, cmp_w2, cmp_pe, w_out, ln_g, ln_b):
    assert w_in.shape[0] == 1, "single-layer decoder"
    b, t, _ = x_prompt.shape
    w_p = _prep_w_in(w_in[0])
    w_out_b = w_out[0].astype(BF16)
    w1p, w2p = _cmp_weights(cmp_w1[0], cmp_w2[0])

    xp = x_prompt.reshape(b * t, D_MODEL)
    tm = 1024 if t % 1024 == 0 else 256
    hg, q, kv, kvb, g, kvt = _inproj(xp, w_p, tm, t_rows=t)
    o_hg, s_p = _hgrn_prompt(hg, hg_lb, hg_norm_g[0], b, t)
    kcv = _cmp_prompt(kv, w1p, w2p, cmp_w1[0], cmp_pe[0], b, t)
    o_nsa = _nsa_prompt(q, kvb, kcv, g, b, t)
    y_p = _outproj(xp, o_hg.reshape(b * t, D_HG), o_nsa.reshape(b * t, D_NSA), w_out_b,
                   ln_g[0], ln_b[0], 512 if (b * t) % 512 == 0 else 256)
    wlen = min(WINDOW, t)
    kv6 = jnp.transpose(kvt.reshape(b, 6, NSA_KV_HEADS, NSA_DH, t), (0, 4, 1, 2, 3))

    y_s, kv_s, win_s, s_s = _sample_path(x_sample, cache_kv[0], cache_win[0], state_hgrn[0], page_table, w_p,
                                         w_out_b, hg_lb, hg_norm_g[0], cmp_w1[0], cmp_w2[0], cmp_pe[0],
                                         ln_g[0], ln_b[0])
    return (y_p.reshape(b, t, D_MODEL), y_s, kv6[None, :, :, :4], kv6[None, :, t - wlen:, 4:],
            s_p[None].astype(x_prompt.dtype), kv_s, win_s, s_s)
```

```python
import functools

import numpy as np
import jax
import jax.numpy as jnp
from jax import lax
from jax.experimental import pallas as pl
from jax.experimental.pallas import tpu as pltpu

F32 = jnp.float32
BF16 = jnp.bfloat16

D_MODEL = 2048
D_HG = 1024
D_NSA = 1024
HG_DK = 128
HG_DV = 128
HG_HEADS = 8
NSA_DH = 64
NSA_HEADS = 16
NSA_KV_HEADS = 4
NSA_GROUP = 4
KV_DIM = 256
CMP_L = 32
CMP_D = 16
CMP_R = 2
CMP_HID = 128
SLC_L = 64
N_SEL = 16
WINDOW = 512
PAGE_SIZE = 128
ALPHA = 2.0 ** 0.25
LN_EPS = 1e-5
RMS_EPS = 1e-6
FORCED_SCORE = 1e4
NEG = -1e30

LANES = 128
SUBLANES = 8
VMEM_LIMIT = 56 * 1024 * 1024
SROWS = 16

COL_TILE = 512
N_HG_TILES = 4 * D_HG // COL_TILE
N_Q_TILES = D_NSA // COL_TILE
N_KV_TILES = 6 * KV_DIM // COL_TILE
G_WIDTH = 1536
N_G_TILES = G_WIDTH // COL_TILE
N_COL_TILES = N_HG_TILES + N_Q_TILES + N_KV_TILES + N_G_TILES
D_IN_PAD = N_COL_TILES * COL_TILE

HG_CHUNK = 128
HG_DIAG = 8
HG_HALVES = tuple(HG_DIAG * 2 ** i for i in range((HG_CHUNK // HG_DIAG).bit_length() - 1))
HG_HPS = 2


def _cparams(sem, flags=None):
    return pltpu.CompilerParams(dimension_semantics=sem, vmem_limit_bytes=VMEM_LIMIT, flags=flags)


def _dot(a, b):
    return jnp.dot(a, b, preferred_element_type=F32)


def _dot_nt(a, b):
    return lax.dot_general(a, b, (((1,), (1,)), ((), ())), preferred_element_type=F32)


def _split3(x):
    hi = x.astype(BF16)
    r1 = x - hi.astype(F32)
    mid = r1.astype(BF16)
    lo = (r1 - mid.astype(F32)).astype(BF16)
    return hi, mid, lo


def _dot3(a_bf, x):
    hi, mid, lo = _split3(x)
    return _dot(a_bf, hi) + _dot(a_bf, mid) + _dot(a_bf, lo)


def _dot3_nt(a_bf, x):
    hi, mid, lo = _split3(x)
    return _dot_nt(a_bf, hi) + _dot_nt(a_bf, mid) + _dot_nt(a_bf, lo)


def _silu(x):
    return x * jax.nn.sigmoid(x)


def _inproj_kernel(x_ref, w_ref, hg_ref, q_ref, kv_ref, kvb_ref, g_ref, *rest, kv_t):
    kvt_ref, kwt_ref, xb_ref = rest if kv_t else (None, None, rest[0])
    n_row_tiles = 4 * KV_DIM // COL_TILE
    j = pl.program_id(1)

    @pl.when(j == 0)
    def _():
        xb_ref[...] = x_ref[...].astype(BF16)

    q0 = N_HG_TILES
    kv0 = q0 + N_Q_TILES
    g0 = kv0 + N_KV_TILES
    tm = x_ref.shape[0]
    n_parts = 2 if tm % (2 * LANES) == 0 else 1
    parts = [slice(h * (tm // n_parts), (h + 1) * (tm // n_parts)) for h in range(n_parts)]

    def emit(store):
        for rs in parts:
            store(rs, _dot(xb_ref[rs, :], w_ref[...]))

    @pl.when(j < q0)
    def _():
        def store(rs, acc):
            hg_ref[rs, :] = acc
        emit(store)

    @pl.when((j >= q0) & (j < kv0))
    def _():
        def store(rs, acc):
            q_ref[rs, :] = acc.astype(BF16)
        emit(store)

    @pl.when((j >= kv0) & (j < g0))
    def _():
        def store(rs, acc):
            kvb_ref[rs, :] = acc.astype(BF16)
            if kv_t:
                acc_t = acc.T

                @pl.when(j < kv0 + n_row_tiles)
                def _():
                    kvt_ref[:, rs] = acc_t

                @pl.when(j >= kv0 + n_row_tiles)
                def _():
                    kwt_ref[:, rs] = acc_t

                @pl.when(j == kv0)
                def _():
                    kv_ref[rs, :] = acc
            else:
                kv_ref[rs, :] = acc
        emit(store)

    @pl.when(j >= g0)
    def _():
        def store(rs, acc):
            g_ref[rs, :] = acc
        emit(store)


def _inproj(x2d, w_p, tm, t_rows=None):
    m = x2d.shape[0]
    assert m % tm == 0
    kv_t = t_rows is not None
    q0 = N_HG_TILES
    kv0 = q0 + N_Q_TILES
    g0 = kv0 + N_KV_TILES

    def clampmap(lo, n):
        return lambda i, j: (i, jnp.clip(j - lo, 0, n - 1))

    out_specs = [pl.BlockSpec((tm, COL_TILE), clampmap(0, N_HG_TILES)),
                 pl.BlockSpec((tm, COL_TILE), clampmap(q0, N_Q_TILES)),
                 pl.BlockSpec((tm, COL_TILE), clampmap(kv0, 1 if kv_t else N_KV_TILES)),
                 pl.BlockSpec((tm, COL_TILE), clampmap(kv0, N_KV_TILES)),
                 pl.BlockSpec((tm, COL_TILE), clampmap(g0, N_G_TILES))]
    out_shape = [jax.ShapeDtypeStruct((m, 4 * D_HG), F32),
                 jax.ShapeDtypeStruct((m, D_NSA), BF16),
                 jax.ShapeDtypeStruct((m, COL_TILE if kv_t else 6 * KV_DIM), F32),
                 jax.ShapeDtypeStruct((m, 6 * KV_DIM), BF16),
                 jax.ShapeDtypeStruct((m, G_WIDTH), F32)]
    if kv_t:
        assert t_rows % tm == 0
        tpb = t_rows // tm
        n_rt = 4 * KV_DIM // COL_TILE
        assert 4 * KV_DIM % COL_TILE == 0 and 2 * KV_DIM == COL_TILE
        out_specs.append(pl.BlockSpec((None, COL_TILE, tm),
                                      lambda i, j: (i // tpb, jnp.clip(j - kv0, 0, n_rt - 1), i % tpb)))
        out_specs.append(pl.BlockSpec((None, COL_TILE, tm), lambda i, j: (i // tpb, 0, i % tpb)))
        out_shape.append(jax.ShapeDtypeStruct((m // t_rows, 4 * KV_DIM, t_rows), F32))
        out_shape.append(jax.ShapeDtypeStruct((m // t_rows, 2 * KV_DIM, t_rows), F32))
    return pl.pallas_call(
        functools.partial(_inproj_kernel, kv_t=kv_t),
        grid=(m // tm, N_COL_TILES),
        in_specs=[pl.BlockSpec((tm, D_MODEL), lambda i, j: (i, 0)),
                  pl.BlockSpec((D_MODEL, COL_TILE), lambda i, j: (0, j))],
        out_specs=out_specs,
        out_shape=out_shape,
        scratch_shapes=[pltpu.VMEM((tm, D_MODEL), BF16)],
        compiler_params=_cparams(("arbitrary", "arbitrary")),
        name="inproj",
    )(x2d, w_p)


def _prep_w_in(w_in):
    o_nq = 4 * D_HG
    o_kv = o_nq + D_NSA
    o_gates = o_kv + 6 * KV_DIM
    o_gate = o_gates + 3 * NSA_HEADS
    pad = G_WIDTH - D_NSA - 3 * NSA_HEADS
    w = jnp.concatenate([w_in[:, :o_gates], w_in[:, o_gate:], w_in[:, o_gates:o_gate],
                         jnp.zeros((D_MODEL, pad), w_in.dtype)], axis=1)
    return w.astype(BF16)


def _outproj_kernel(x_ref, ohg_ref, onsa_ref, w_ref, g_ref, b_ref, y_ref):
    tm = x_ref.shape[0]
    n_parts = 2 if tm % (2 * LANES) == 0 else 1
    for h in range(n_parts):
        rs = slice(h * (tm // n_parts), (h + 1) * (tm // n_parts))
        y = _dot(ohg_ref[rs, :], w_ref[0:D_HG, :]) + _dot(onsa_ref[rs, :], w_ref[D_HG:D_MODEL, :])
        z = ALPHA * x_ref[rs, :] + y
        mu = jnp.mean(z, axis=-1, keepdims=True)
        zc = z - mu
        var = jnp.mean(zc * zc, axis=-1, keepdims=True)
        y_ref[rs, :] = zc * lax.rsqrt(var + LN_EPS) * g_ref[...] + b_ref[...]


def _outproj(x2d, o_hg, o_nsa, w_out_b, ln_g, ln_b, tm):
    m = x2d.shape[0]
    assert m % tm == 0
    return pl.pallas_call(
        _outproj_kernel,
        grid=(m // tm,),
        in_specs=[pl.BlockSpec((tm, D_MODEL), lambda i: (i, 0)),
                  pl.BlockSpec((tm, D_HG), lambda i: (i, 0)),
                  pl.BlockSpec((tm, D_NSA), lambda i: (i, 0)),
                  pl.BlockSpec((D_MODEL, D_MODEL), lambda i: (0, 0)),
                  pl.BlockSpec((1, D_MODEL), lambda i: (0, 0)),
                  pl.BlockSpec((1, D_MODEL), lambda i: (0, 0))],
        out_specs=pl.BlockSpec((tm, D_MODEL), lambda i: (i, 0)),
        out_shape=jax.ShapeDtypeStruct((m, D_MODEL), F32),
        compiler_params=_cparams(("arbitrary",)),
        name="outproj_ln",
    )(x2d, o_hg, o_nsa, w_out_b, ln_g.reshape(1, D_MODEL), ln_b.reshape(1, D_MODEL))


def _hgrn_gmat():
    c = HG_CHUNK
    tri = np.tril(np.ones((c, c), np.float32))
    mats = [tri]
    for half in HG_HALVES:
        t = np.arange(c)
        mid = (t // (2 * half)) * (2 * half) + half
        mats.append(tri - tri[mid - 1])
    return jnp.asarray(np.concatenate(mats, axis=0), BF16)


def _hgrn_lower_bound(lb_ref):
    raw = lb_ref[...]
    mx = jnp.max(raw, axis=0, keepdims=True)
    e = jnp.exp(raw - mx)
    return e[0:1, :] / jnp.sum(e, axis=0, keepdims=True)


def _hgrn_kernel(lb_ref, ng_ref, gm_ref, q_ref, f_ref, i_ref, g_ref, o_ref, s_ref, st_ref):
    c = HG_CHUNK
    t_len = q_ref.shape[0]
    lb2 = _hgrn_lower_bound(lb_ref)
    st_ref[...] = jnp.zeros_like(st_ref)

    row = lax.broadcasted_iota(jnp.int32, (c, HG_DK), 0)
    sub3 = lax.broadcasted_iota(jnp.int32, (c // SUBLANES, SUBLANES, HG_DK), 1)
    r2 = lax.broadcasted_iota(jnp.int32, (c, c), 0)
    c2 = lax.broadcasted_iota(jnp.int32, (c, c), 1)

    n_chunks = t_len // c

    def gates(ci):
        fr2 = f_ref[pl.ds(pl.multiple_of(ci * c, c), c), :]
        logf2 = jnp.log(lb2 + (1.0 - lb2) * jax.nn.sigmoid(fr2))
        return (1.0 - lb2) * jax.nn.sigmoid(-fr2), _dot3(gm_ref[...], logf2)

    def chunk(ci, carry):
        k2, dall2 = carry
        nxt = gates(jnp.minimum(ci + 1, n_chunks - 1))
        sl = pl.ds(pl.multiple_of(ci * c, c), c)
        for hh in range(HG_HPS):
            hs = slice(hh * HG_DK, (hh + 1) * HG_DK)
            one_head(sl, hh, q_ref[sl, hs], k2[:, hs], i_ref[sl, hs], dall2[:, hs], g_ref[sl, hs],
                     ng_ref[:, hs])
        return nxt

    def one_head(sl, hh, q, k, v, dall, gate, ng):
        a = dall[0:c]

        q3 = q.reshape(c // SUBLANES, SUBLANES, HG_DK)
        k3 = k.reshape(c // SUBLANES, SUBLANES, HG_DK)
        v3 = v.reshape(c // SUBLANES, SUBLANES, HG_DV)
        a3 = a.reshape(c // SUBLANES, SUBLANES, HG_DK)
        od = jnp.sum(q3 * k3, axis=-1, keepdims=True) * v3
        for d in range(1, HG_DIAG):
            ks = pltpu.roll(k3, d, 1)
            as_ = pltpu.roll(a3, d, 1)
            vs = pltpu.roll(v3, d, 1)
            ok = (sub3 % HG_DIAG) >= d
            w = jnp.where(ok, q3 * ks * jnp.exp(jnp.minimum(a3 - as_, 0.0)), 0.0)
            od = od + jnp.sum(w, axis=-1, keepdims=True) * vs
        od = od.reshape(c, HG_DV)

        att = jnp.zeros((c, c), F32)
        for lvl, half in enumerate(HG_HALVES, start=1):
            dl = dall[lvl * c:(lvl + 1) * c]
            e = jnp.exp(-jnp.abs(dl))
            upper = (row % (2 * half)) >= half
            qt = jnp.where(upper, q * e, 0.0).astype(BF16)
            kt = jnp.where(upper, 0.0, k * e).astype(BF16)
            al = _dot_nt(qt, kt)
            if 2 * half < c:
                al = jnp.where((r2 // (2 * half)) == (c2 // (2 * half)), al, 0.0)
            att = att + al
        vb = v.astype(BF16)
        o = od + _dot(att.astype(BF16), vb)

        st = st_ref[hh]
        o = o + _dot_nt((q * jnp.exp(a)).astype(BF16), st.astype(BF16))
        alast = a[c - 1:c, :]
        kd = (k * jnp.exp(alast - a)).astype(BF16)
        st_ref[hh] = st * jnp.exp(alast) + _dot(v.T.astype(BF16), kd)

        o = o * lax.rsqrt(jnp.mean(o * o, axis=-1, keepdims=True) + RMS_EPS)
        o_ref[sl, hh * HG_DV:(hh + 1) * HG_DV] = (o * ng * _silu(gate)).astype(o_ref.dtype)

    lax.fori_loop(0, n_chunks, chunk, gates(0))
    for hh in range(HG_HPS):
        s_ref[hh] = st_ref[hh].T


def _hgrn_prompt(hg, hg_lb, norm_g, b, t):
    assert t % HG_CHUNK == 0
    hg3 = hg.reshape(b, t, 4 * D_HG)
    gm = _hgrn_gmat()

    n_hb = HG_HEADS // HG_HPS
    wide = HG_HPS * HG_DK

    def col(k):
        return pl.BlockSpec((None, t, wide), lambda bi, h: (bi, 0, k * n_hb + h))

    return pl.pallas_call(
        _hgrn_kernel,
        grid=(b, n_hb),
        in_specs=[pl.BlockSpec((2, wide), lambda bi, h: (0, h)),
                  pl.BlockSpec((1, wide), lambda bi, h: (0, h)),
                  pl.BlockSpec(gm.shape, lambda bi, h: (0, 0)),
                  col(0), col(1), col(2), col(3)],
        out_specs=[pl.BlockSpec((None, t, wide), lambda bi, h: (bi, 0, h)),
                   pl.BlockSpec((None, HG_HPS, HG_DK, HG_DV), lambda bi, h: (bi, h, 0, 0))],
        out_shape=[jax.ShapeDtypeStruct((b, t, D_HG), BF16),
                   jax.ShapeDtypeStruct((b, HG_HEADS, HG_DK, HG_DV), F32)],
        scratch_shapes=[pltpu.VMEM((HG_HPS, HG_DV, HG_DK), F32)],
        compiler_params=_cparams(("arbitrary", "arbitrary")),
        name="hgrn_prompt",
    )(hg_lb, norm_g.reshape(1, D_HG), gm, hg3, hg3, hg3, hg3)


def _hgrn_step_kernel(lb_ref, ng_ref, x_ref, s_ref, o_ref, so_ref):
    raw = lb_ref[...]
    mx = jnp.max(raw, axis=0)
    e0 = jnp.exp(raw[0] - mx)
    lb_all = e0 / (e0 + jnp.exp(raw[1] - mx))
    x = x_ref[...]
    eye = (lax.broadcasted_iota(jnp.int32, (HG_DK, HG_DK), 0)
           == lax.broadcasted_iota(jnp.int32, (HG_DK, HG_DK), 1))

    def column(rowvec):
        return jnp.sum(jnp.where(eye, rowvec, 0.0), axis=1, keepdims=True)

    for h in range(HG_HEADS):
        lb = lb_all[h:h + 1]
        q = x[h:h + 1]
        fr = x[HG_HEADS + h:HG_HEADS + h + 1]
        v = x[2 * HG_HEADS + h:2 * HG_HEADS + h + 1]
        gate = x[3 * HG_HEADS + h:3 * HG_HEADS + h + 1]
        f = lb + (1.0 - lb) * jax.nn.sigmoid(fr)
        k = (1.0 - lb) * jax.nn.sigmoid(-fr)
        s_new = s_ref[h] * column(f) + column(k) * v
        so_ref[h] = s_new
        o = jnp.sum(column(q) * s_new, axis=0, keepdims=True)
        o = o * lax.rsqrt(jnp.mean(o * o, axis=-1, keepdims=True) + RMS_EPS)
        o_ref[h:h + 1, :] = o * ng_ref[h:h + 1, :] * _silu(gate)


def _hgrn_sample(hg_s, state, hg_lb, norm_g):
    b = hg_s.shape[0]
    x3 = hg_s.reshape(b, 4 * HG_HEADS, HG_DK)
    return pl.pallas_call(
        _hgrn_step_kernel,
        grid=(b,),
        in_specs=[pl.BlockSpec((2, HG_HEADS, HG_DK), lambda i: (0, 0, 0)),
                  pl.BlockSpec((HG_HEADS, HG_DV), lambda i: (0, 0)),
                  pl.BlockSpec((None, 4 * HG_HEADS, HG_DK), lambda i: (i, 0, 0)),
                  pl.BlockSpec((None, HG_HEADS, HG_DK, HG_DV), lambda i: (i, 0, 0, 0))],
        out_specs=[pl.BlockSpec((None, HG_HEADS, HG_DV), lambda i: (i, 0, 0)),
                   pl.BlockSpec((None, HG_HEADS, HG_DK, HG_DV), lambda i: (i, 0, 0, 0))],
        out_shape=[jax.ShapeDtypeStruct((b, HG_HEADS, HG_DV), F32),
                   jax.ShapeDtypeStruct((b, HG_HEADS, HG_DK, HG_DV), F32)],
        compiler_params=_cparams(("arbitrary",)),
        name="hgrn_sample",
    )(hg_lb.reshape(2, HG_HEADS, HG_DK), norm_g.reshape(HG_HEADS, HG_DV), x3, state)


def _cmp_weights(cmp_w1, cmp_w2):
    w1 = cmp_w1.reshape(2, CMP_R, CMP_D, NSA_DH, CMP_HID)
    w1 = jnp.transpose(w1, (0, 2, 3, 1, 4)).reshape(2, CMP_D, NSA_DH, CMP_R * CMP_HID)
    z1 = jnp.zeros_like(w1)
    w1p = jnp.stack([jnp.concatenate([w1, z1], axis=2), jnp.concatenate([z1, w1], axis=2)], axis=1)
    z2 = jnp.zeros_like(cmp_w2)
    w2p = jnp.stack([jnp.concatenate([cmp_w2, z2], axis=2), jnp.concatenate([z2, cmp_w2], axis=2)], axis=1)
    return w1p.astype(BF16), w2p.astype(BF16)


def _pe_term(pe_ref, w1f_ref):
    pe = jnp.broadcast_to(pe_ref[...], (SUBLANES, CMP_L * NSA_DH))
    w = w1f_ref[...]
    wh, wm, wl = _split3(w)
    ph, pm, pl_ = _split3(pe)
    out = (_dot(ph, wh) + _dot(ph, wm) + _dot(pm, wh)
           + _dot(ph, wl) + _dot(pm, wm) + _dot(pl_, wh))
    return out[0:1, :]


def _cmp_prompt_kernel(x_ref, w1p_ref, w2p_ref, pe_ref, w1f_ref, o_ref):
    n_sub = x_ref.shape[0] // CMP_D
    pe_t = _pe_term(pe_ref, w1f_ref)
    out = jnp.zeros((n_sub, LANES), F32)
    for par in range(2):
        acc = jnp.zeros((n_sub, CMP_R * CMP_HID), F32)
        for j in range(CMP_D):
            xj = x_ref[pl.ds(j, n_sub, stride=CMP_D), :].astype(BF16)
            acc = acc + _dot(xj, w1p_ref[par, j])
        nxt = pltpu.roll(acc[:, CMP_HID:], n_sub - 1, 0)
        hid = pe_t + acc[:, :CMP_HID] + nxt
        out = out + _dot(_silu(hid).astype(BF16), w2p_ref[par])
    o_ref[...] = out.astype(o_ref.dtype)


def _cmp_prompt(kv, w1p, w2p, cmp_w1, cmp_pe, b, t):
    assert t % CMP_D == 0 and (t // CMP_D) % SUBLANES == 0
    n_sub = t // CMP_D
    kv3 = kv.reshape(b, t, kv.shape[-1])
    pe = cmp_pe.reshape(2, 1, CMP_L * NSA_DH)
    w1f = cmp_w1.reshape(2, CMP_L * NSA_DH, CMP_HID)
    return pl.pallas_call(
        _cmp_prompt_kernel,
        grid=(b, 2, 2),
        in_specs=[pl.BlockSpec((None, t, LANES), lambda bi, w, p: (bi, 0, 2 * w + p)),
                  pl.BlockSpec((None, 2, CMP_D, LANES, CMP_R * CMP_HID), lambda bi, w, p: (w, 0, 0, 0, 0)),
                  pl.BlockSpec((None, 2, CMP_HID, LANES), lambda bi, w, p: (w, 0, 0, 0)),
                  pl.BlockSpec((None, 1, CMP_L * NSA_DH), lambda bi, w, p: (w, 0, 0)),
                  pl.BlockSpec((None, CMP_L * NSA_DH, CMP_HID), lambda bi, w, p: (w, 0, 0))],
        out_specs=pl.BlockSpec((None, None, n_sub, LANES), lambda bi, w, p: (bi, w, 0, p)),
        out_shape=jax.ShapeDtypeStruct((b, 2, n_sub, KV_DIM), BF16),
        compiler_params=_cparams(("arbitrary", "arbitrary", "arbitrary")),
        name="cmp_prompt",
    )(kv3, w1p, w2p, pe, w1f)


TQ = 256
SLC_TK = 512


def _overlap_t(n_slc, n_cmp_pad):
    ci = np.arange(n_cmp_pad)[None, :] * CMP_D
    sj = np.arange(n_slc)[:, None] * SLC_L
    return ((ci < sj + SLC_L) & (ci + CMP_L > sj)).astype(np.float32)


def _gate_expand(pair):
    m = np.zeros((3, LANES, 2 * NSA_GROUP * NSA_DH), np.float32)
    for br in range(3):
        for hp in range(2):
            for g in range(NSA_GROUP):
                src = br * NSA_HEADS + (2 * pair + hp) * NSA_GROUP + g
                c0 = (hp * NSA_GROUP + g) * NSA_DH
                m[br, src, c0:c0 + NSA_DH] = 1.0
    return m


def _place_pair(x_even, x_odd, hp, lane):
    if hp == 0:
        return jnp.where(lane < NSA_DH, x_even, pltpu.roll(x_odd, NSA_DH, 1))
    return jnp.where(lane < NSA_DH, pltpu.roll(x_even, NSA_DH, 1), x_odd)


def _nsa_prompt_kernel(q_ref, sk_ref, sv_ref, wk_ref, wv_ref, kc_ref, vc_ref, gate_ref, gates_ref,
                       ovt_ref, gex_ref, o_ref, ka_ref, va_ref, wka_ref, wva_ref):
    i = pl.program_id(2)
    tq = q_ref.shape[0]
    t_len = sk_ref.shape[0]
    n_slc = ovt_ref.shape[0]
    n_cmp_pad = kc_ref.shape[0]
    rows = 2 * NSA_GROUP * tq
    n_sel = min(N_SEL, n_slc)

    @pl.when(i == 0)
    def _():
        ones = jnp.ones((t_len, LANES), BF16)
        jl = lax.broadcasted_iota(jnp.int32, (t_len, LANES), 1)
        ka_ref[:, 0:LANES] = sk_ref[...]
        kb = lax.broadcasted_iota(jnp.int32, (t_len, LANES), 0) // SLC_L
        ka_ref[:, LANES:2 * LANES] = jnp.where(kb == jl, 1.0, 0.0).astype(BF16)
        va_ref[0] = jnp.where(jl < NSA_DH, sv_ref[...], ones)
        va_ref[1] = jnp.where(jl < NSA_DH, ones, sv_ref[...])
        jp = lax.broadcasted_iota(jnp.int32, (WINDOW, 2 * LANES), 1)
        wka_ref[0:WINDOW, :] = jnp.where(jp == LANES, NEG, 0.0).astype(BF16)
        wka_ref[WINDOW:WINDOW + t_len, 0:LANES] = wk_ref[...]
        wka_ref[WINDOW:WINDOW + t_len, LANES:2 * LANES] = jnp.zeros((t_len, LANES), BF16)
        for hp in range(2):
            wva_ref[hp, 0:WINDOW, :] = jnp.zeros((WINDOW, LANES), BF16)
        wva_ref[0, WINDOW:WINDOW + t_len, :] = jnp.where(jl < NSA_DH, wv_ref[...], ones)
        wva_ref[1, WINDOW:WINDOW + t_len, :] = jnp.where(jl < NSA_DH, ones, wv_ref[...])

    lane = lax.broadcasted_iota(jnp.int32, (tq, LANES), 1)
    qpos_r = i * tq + lax.broadcasted_iota(jnp.int32, (rows, 1), 0) % tq
    sig_gates = jax.nn.sigmoid(gates_ref[...])

    qs = []
    for hp in range(2):
        keep = (lane >= NSA_DH) if hp == 1 else (lane < NSA_DH)
        for g in range(NSA_GROUP):
            blk = hp * 2 + g // 2
            x = q_ref[:, blk * LANES:(blk + 1) * LANES]
            if g % 2 != hp:
                x = pltpu.roll(x, NSA_DH, 1)
            qs.append(jnp.where(keep, x, jnp.zeros_like(x)))
    qs = jnp.concatenate(qs, axis=0) * jnp.asarray(NSA_DH ** -0.5, BF16)

    s = _dot_nt(qs, kc_ref[...])
    end = lax.broadcasted_iota(jnp.int32, (rows, n_cmp_pad), 1) * CMP_D + (CMP_L - 1)
    msk = end <= qpos_r
    s = jnp.where(msk, s, NEG)
    m = jnp.max(s, axis=-1, keepdims=True)
    p = jnp.where(msk, jnp.exp(s - m), 0.0)
    den = jnp.sum(p, axis=-1, keepdims=True)
    p = p / jnp.where(den > 0, den, 1.0)
    o_c = _dot(p.astype(BF16), vc_ref[...])

    jb = lax.broadcasted_iota(jnp.int32, (n_slc, tq), 0)
    qpos_l = i * tq + lax.broadcasted_iota(jnp.int32, (n_slc, tq), 1)
    cur = qpos_l // SLC_L
    forced = (jb == 0) | (jb == cur) | (jb == cur - 1)
    valid = jb * SLC_L <= qpos_l
    selfeat = []
    for hp in range(2):
        r0 = hp * NSA_GROUP * tq
        psum = p[r0:r0 + tq] + p[r0 + tq:r0 + 2 * tq] + p[r0 + 2 * tq:r0 + 3 * tq] + p[r0 + 3 * tq:r0 + 4 * tq]
        imp_t = _dot3_nt(ovt_ref[...], psum)
        score = jnp.where(forced, FORCED_SCORE, jnp.where(valid, imp_t, -1.0))
        cnt = jnp.zeros((n_slc, tq), jnp.int32)
        for jp in range(n_slc):
            r = score[jp:jp + 1, :]
            beats = (r > score) | ((r == score) & (jb > jp))
            cnt = cnt + beats.astype(jnp.int32)
        self_t = jnp.where(cnt < n_sel, 0.0, NEG)
        if n_slc < LANES:
            self_t = jnp.concatenate([self_t, jnp.zeros((LANES - n_slc, tq), F32)], axis=0)
        selfeat += [self_t.T.astype(BF16)] * NSA_GROUP
    qa = jnp.concatenate([qs, jnp.concatenate(selfeat, axis=0)], axis=1)

    tk = min(SLC_TK, t_len)

    hrows = rows // 2
    qa_h = (qa[:hrows], qa[hrows:])
    qpos_h = qpos_r[:hrows]

    def slc_tile(kt, carry, causal):
        ks = pl.ds(pl.multiple_of(kt * tk, tk), tk)
        out = []
        for hp in range(2):
            m_i, acc = carry[hp]
            s2 = _dot_nt(qa_h[hp], ka_ref[ks, :])
            if causal:
                kpos = kt * tk + lax.broadcasted_iota(jnp.int32, (hrows, tk), 1)
                s2 = jnp.where(kpos <= qpos_h, s2, NEG)
            m_n = jnp.maximum(m_i, jnp.max(s2, axis=-1, keepdims=True))
            al = jnp.exp(m_i - m_n)
            p2 = jnp.exp((s2 - m_n).astype(BF16))
            out.append((m_n, al * acc + _dot(p2, va_ref[hp, ks, :])))
        return tuple(out)

    lane_h = lax.broadcasted_iota(jnp.int32, (hrows, LANES), 1)

    def normalise(acc, hp):
        head = (lane_h >= NSA_DH) if hp == 1 else (lane_h < NSA_DH)
        return acc / jnp.where(head, pltpu.roll(acc, NSA_DH, 1), 1.0)

    n_full = (i * tq) // tk
    init = (jnp.full((hrows, 1), NEG, F32), jnp.zeros((hrows, LANES), F32))
    carry = lax.fori_loop(0, n_full, functools.partial(slc_tile, causal=False), (init, init))
    carry = slc_tile(n_full, carry, True)
    o_s = jnp.concatenate([normalise(carry[0][1], 0), normalise(carry[1][1], 1)], axis=0)

    lw = WINDOW + tq
    ws = pl.ds(pl.multiple_of(i * tq, tq), lw)
    qw = jnp.concatenate([qs, jnp.where(lax.broadcasted_iota(jnp.int32, (rows, LANES), 1) == 0, 1.0, 0.0
                                        ).astype(BF16)], axis=1)
    cr = (lax.broadcasted_iota(jnp.int32, (hrows, tq), 1)
          - lax.broadcasted_iota(jnp.int32, (hrows, tq), 0) % tq)
    o_w = []
    for hp in range(2):
        s3 = _dot_nt(qw[hp * hrows:(hp + 1) * hrows], wka_ref[ws, :])
        s3 = jnp.concatenate([jnp.where(cr > 0, s3[:, :tq], NEG), s3[:, tq:lw - tq],
                              jnp.where(cr <= 0, s3[:, lw - tq:], NEG)], axis=1)
        m3 = jnp.max(s3, axis=-1, keepdims=True)
        p3 = jnp.exp((s3 - m3).astype(BF16))
        o_w.append(normalise(_dot(p3, wva_ref[hp, ws, :]), hp))
    o_w = jnp.concatenate(o_w, axis=0)

    sig_hi = sig_gates.astype(BF16)
    sig_lo = (sig_gates - sig_hi.astype(F32)).astype(BF16)
    out = None
    for br, ob in enumerate((o_c, o_s, o_w)):
        parts = []
        for hp in range(2):
            for gp in range(2):
                r0 = (hp * NSA_GROUP + 2 * gp) * tq
                parts.append(_place_pair(ob[r0:r0 + tq], ob[r0 + tq:r0 + 2 * tq], hp, lane))
        gexp = _dot(sig_hi, gex_ref[br]) + _dot(sig_lo, gex_ref[br])
        term = gexp * jnp.concatenate(parts, axis=1)
        out = term if out is None else out + term
    o_ref[...] = (out * _silu(gate_ref[...])).astype(o_ref.dtype)


def _nsa_prompt(q, kvb, kcv, g, b, t):
    assert t % TQ == 0 and t % SLC_L == 0 and (t <= SLC_TK or t % SLC_TK == 0)
    n_sub = t // CMP_D
    n_slc = t // SLC_L
    assert n_slc <= LANES
    q3 = q.reshape(b, t, D_NSA)
    kvb3 = kvb.reshape(b, t, 6 * KV_DIM)
    g3 = g.reshape(b, t, G_WIDTH)
    ovt = jnp.asarray(_overlap_t(n_slc, n_sub), BF16)
    gex = jnp.asarray(np.stack([_gate_expand(0), _gate_expand(1)]), BF16)
    pw = 2 * NSA_GROUP * NSA_DH

    def kvcol(which):
        return pl.BlockSpec((None, t, LANES), lambda bi, p, i: (bi, 0, 2 * which + p))

    return pl.pallas_call(
        _nsa_prompt_kernel,
        grid=(b, 2, t // TQ),
        in_specs=[pl.BlockSpec((None, TQ, pw), lambda bi, p, i: (bi, i, p)),
                  kvcol(2), kvcol(3), kvcol(4), kvcol(5),
                  pl.BlockSpec((None, None, n_sub, LANES), lambda bi, p, i: (bi, 0, 0, p)),
                  pl.BlockSpec((None, None, n_sub, LANES), lambda bi, p, i: (bi, 1, 0, p)),
                  pl.BlockSpec((None, TQ, pw), lambda bi, p, i: (bi, i, p)),
                  pl.BlockSpec((None, TQ, LANES), lambda bi, p, i: (bi, i, D_NSA // LANES)),
                  pl.BlockSpec(ovt.shape, lambda bi, p, i: (0, 0)),
                  pl.BlockSpec((None, 3, LANES, pw), lambda bi, p, i: (p, 0, 0, 0))],
        out_specs=pl.BlockSpec((None, TQ, pw), lambda bi, p, i: (bi, i, p)),
        out_shape=jax.ShapeDtypeStruct((b, t, D_NSA), BF16),
        scratch_shapes=[pltpu.VMEM((t, 2 * LANES), BF16), pltpu.VMEM((2, t, LANES), BF16),
                        pltpu.VMEM((t + WINDOW, 2 * LANES), BF16), pltpu.VMEM((2, t + WINDOW, LANES), BF16)],
        compiler_params=_cparams(("arbitrary", "arbitrary", "arbitrary")),
        name="nsa_prompt",
    )(q3, kvb3, kvb3, kvb3, kvb3, kcv, kcv, g3, g3, ovt, gex)


CMP_PAGES = 32


def _cmp_bd_weights(cmp_w1, cmp_w2):
    w1 = cmp_w1.reshape(2, CMP_R, CMP_D, NSA_DH, CMP_HID)
    w1 = jnp.transpose(w1, (0, 2, 3, 1, 4)).reshape(2, CMP_D, NSA_DH, CMP_R * CMP_HID)
    eye2 = jnp.eye(2, dtype=w1.dtype)
    bd = jnp.einsum('wjdn,hg->wjhdgn', w1, eye2).reshape(2, CMP_D, LANES, 2 * CMP_R * CMP_HID)
    eye4 = jnp.eye(NSA_KV_HEADS, dtype=w1.dtype)
    w2q = jnp.einsum('wed,hg->whegd', cmp_w2, eye4).reshape(2, NSA_KV_HEADS, CMP_HID, KV_DIM)
    return bd.astype(BF16), w2q.astype(BF16)


def _cmp_sample_kernel(pt_ref, cache_ref, perm_ref, bd_ref, w2q_ref, pe_ref, w1f_ref, o_ref, tbuf_ref, buf_ref,
                       acc_ref, sem_ref, *, pages, n_b, n_c):
    w = pl.program_id(0)
    b = pl.program_id(1)
    c = pl.program_id(2)
    step = (w * n_b + b) * n_c + c
    total = 2 * n_b * n_c
    slot = step % 2
    rows = pages * PAGE_SIZE // CMP_D

    def copies(w_, b_, c_, slot_):
        return [pltpu.make_async_copy(
            cache_ref.at[pt_ref[b_, c_ * pages + p], pl.ds(pl.multiple_of(w_ * KV_DIM, KV_DIM), KV_DIM), :],
            tbuf_ref.at[slot_, p], sem_ref.at[slot_]) for p in range(pages)]

    @pl.when(step == 0)
    def _():
        for cp in copies(w, b, c, slot):
            cp.start()

    @pl.when(step + 1 < total)
    def _():
        nxt = step + 1
        c_n = nxt % n_c
        b_n = (nxt // n_c) % n_b
        w_n = nxt // (n_c * n_b)
        for cp in copies(w_n, b_n, c_n, 1 - slot):
            cp.start()

    for cp in copies(w, b, c, slot):
        cp.wait()

    def to_rows(p, carry):
        xp = _dot_nt(perm_ref[...], tbuf_ref[slot, p].astype(BF16))
        for half in range(2):
            buf_ref[half, p] = xp[:, half * LANES:(half + 1) * LANES].reshape(CMP_D, n_pp, LANES)
        return carry

    n_pp = PAGE_SIZE // CMP_D
    lax.fori_loop(0, pages, to_rows, 0, unroll=8 if pages % 8 == 0 else 1)

    pw = 2 * CMP_R * CMP_HID
    for half in range(2):
        acc = jnp.zeros((rows, pw), F32)
        for jp in range(CMP_D // 2):
            xa = buf_ref[half, :, 2 * jp].reshape(rows, LANES)
            xb = buf_ref[half, :, 2 * jp + 1].reshape(rows, LANES)
            acc = acc + _dot(jnp.concatenate([xa, xb], axis=1).astype(BF16), bd_ref[jp])
        acc_ref[pl.ds(pl.multiple_of(c * rows, rows), rows), half * pw:(half + 1) * pw] = acc

    @pl.when(c == n_c - 1)
    def _():
        n_sub = acc_ref.shape[0]
        pe_t = _pe_term(pe_ref, w1f_ref)
        out = jnp.zeros((n_sub, KV_DIM), F32)
        for h in range(NSA_KV_HEADS):
            c0 = h * CMP_R * CMP_HID
            nxt = pltpu.roll(acc_ref[:, c0 + CMP_HID:c0 + 2 * CMP_HID], n_sub - 1, 0)
            hid = pe_t + acc_ref[:, c0:c0 + CMP_HID] + nxt
            out = out + _dot(_silu(hid).astype(BF16), w2q_ref[h])
        o_ref[...] = out.astype(o_ref.dtype)


def _cmp_sample(cache_t, page_table, bd, w2q, cmp_w1, cmp_pe):
    n_b, n_pages = page_table.shape
    pages = min(CMP_PAGES, n_pages)
    assert n_pages % pages == 0
    n_c = n_pages // pages
    n_sub = n_pages * PAGE_SIZE // CMP_D
    pe = cmp_pe.reshape(2, 1, CMP_L * NSA_DH)
    w1f = cmp_w1.reshape(2, CMP_L * NSA_DH, CMP_HID)
    nbd = NSA_KV_HEADS * CMP_R * CMP_HID
    n_pp = PAGE_SIZE // CMP_D
    perm = np.zeros((PAGE_SIZE, PAGE_SIZE), np.float32)
    jn = np.arange(PAGE_SIZE)
    perm[jn, (jn % n_pp) * CMP_D + jn // n_pp] = 1.0
    perm = jnp.asarray(perm, BF16)
    bd2 = bd.reshape(2, CMP_D // 2, 2 * LANES, nbd // 2)
    kern = functools.partial(_cmp_sample_kernel, pages=pages, n_b=n_b, n_c=n_c)
    return pl.pallas_call(
        kern,
        grid_spec=pltpu.PrefetchScalarGridSpec(
            num_scalar_prefetch=1,
            grid=(2, n_b, n_c),
            in_specs=[pl.BlockSpec(memory_space=pl.ANY),
                      pl.BlockSpec((PAGE_SIZE, PAGE_SIZE), lambda w, b, c, pt: (0, 0)),
                      pl.BlockSpec((None, CMP_D // 2, 2 * LANES, nbd // 2), lambda w, b, c, pt: (w, 0, 0, 0)),
                      pl.BlockSpec((None, NSA_KV_HEADS, CMP_HID, KV_DIM), lambda w, b, c, pt: (w, 0, 0, 0)),
                      pl.BlockSpec((None, 1, CMP_L * NSA_DH), lambda w, b, c, pt: (w, 0, 0)),
                      pl.BlockSpec((None, CMP_L * NSA_DH, CMP_HID), lambda w, b, c, pt: (w, 0, 0))],
            out_specs=pl.BlockSpec((None, None, n_sub, KV_DIM), lambda w, b, c, pt: (b, w, 0, 0)),
            scratch_shapes=[pltpu.VMEM((2, pages, KV_DIM, PAGE_SIZE), F32),
                            pltpu.VMEM((2, pages, CMP_D, n_pp, LANES), F32),
                            pltpu.VMEM((n_sub, nbd), F32),
                            pltpu.SemaphoreType.DMA((2,))]),
        out_shape=jax.ShapeDtypeStruct((n_b, 2, n_sub, KV_DIM), BF16),
        compiler_params=_cparams(("arbitrary", "arbitrary", "arbitrary")),
        name="cmp_sample",
    )(page_table, cache_t, perm, bd2, w2q, pe, w1f)


def _nsa_sample_a_kernel(qh_ref, kc_ref, vc_ref, ov_ref, oc_ref, sel_ref, *, past_len, n_slc):
    n_sub = kc_ref.shape[0]
    n_slc_pad = ov_ref.shape[1]
    n_sel = min(N_SEL, n_slc)
    qpos = past_len
    rowi = lax.broadcasted_iota(jnp.int32, (SUBLANES, n_sub), 0)
    rowq = lax.broadcasted_iota(jnp.int32, (SROWS, n_sub), 0)
    ncol = lax.broadcasted_iota(jnp.int32, (SROWS, n_sub), 1)
    msk = (ncol * CMP_D + (CMP_L - 1) <= qpos) & (ncol < n_sub - 1)
    psum = jnp.zeros((SUBLANES, n_sub), F32)
    for h in range(NSA_KV_HEADS):
        pr = h // 2
        qh = (qh_ref[h] * NSA_DH ** -0.5).astype(BF16)
        s = _dot_nt(qh, kc_ref[:, pr * LANES:(pr + 1) * LANES])
        s = jnp.where(msk, s, NEG)
        m = jnp.max(s, axis=-1, keepdims=True)
        p = jnp.where(msk, jnp.exp(s - m), 0.0)
        den = jnp.sum(p, axis=-1, keepdims=True)
        p = p / jnp.where(den > 0, den, 1.0)
        oc_ref[h] = _dot(p.astype(BF16), vc_ref[:, pr * LANES:(pr + 1) * LANES])
        ph = jnp.sum(jnp.where(rowq < NSA_GROUP, p, 0.0), axis=0, keepdims=True)
        psum = jnp.where(rowi == h, ph, psum)

    hi, mid, lo = _split3(psum)
    imp = _dot(hi, ov_ref[...]) + _dot(mid, ov_ref[...]) + _dot(lo, ov_ref[...])
    jb = lax.broadcasted_iota(jnp.int32, (SUBLANES, n_slc_pad), 1)
    cur = qpos // SLC_L
    forced = (jb == 0) | (jb == cur) | (jb == cur - 1)
    valid = jb * SLC_L <= qpos
    score = jnp.where(forced, FORCED_SCORE, jnp.where(valid, imp, -1.0))
    score = jnp.where(jb < n_slc, score, -2.0)
    cnt = jnp.zeros((SUBLANES, n_slc_pad), jnp.int32)
    for jp in range(n_slc):
        r = score[:, jp:jp + 1]
        beats = (r > score) | ((r == score) & (jb > jp))
        cnt = cnt + beats.astype(jnp.int32)
    cnt = jnp.where(jb < n_slc, cnt, n_slc_pad)
    lane = lax.broadcasted_iota(jnp.int32, (SUBLANES, LANES), 1)
    idx = jnp.zeros((SUBLANES, LANES), jnp.int32)
    for slot in range(n_sel):
        pick = jnp.sum(jnp.where(cnt == slot, jb, 0), axis=1, keepdims=True)
        idx = jnp.where(lane == slot, pick, idx)
    sel_ref[...] = idx


def _nsa_sample_a(qh, kcv, past_len):
    n_b = qh.shape[0]
    n_sub = kcv.shape[2]
    n_slc = -(-(past_len + 1) // SLC_L)
    n_slc_pad = -(-n_slc // LANES) * LANES
    ov = np.zeros((n_sub, n_slc_pad), np.float32)
    ov[:, :n_slc] = _overlap_t(n_slc, n_sub).T
    ov = jnp.asarray(ov, BF16)
    kern = functools.partial(_nsa_sample_a_kernel, past_len=past_len, n_slc=n_slc)
    return pl.pallas_call(
        kern,
        grid=(n_b,),
        in_specs=[pl.BlockSpec((None, NSA_KV_HEADS, SROWS, LANES), lambda b: (b, 0, 0, 0)),
                  pl.BlockSpec((None, None, n_sub, KV_DIM), lambda b: (b, 0, 0, 0)),
                  pl.BlockSpec((None, None, n_sub, KV_DIM), lambda b: (b, 1, 0, 0)),
                  pl.BlockSpec(ov.shape, lambda b: (0, 0))],
        out_specs=[pl.BlockSpec((None, NSA_KV_HEADS, SROWS, LANES), lambda b: (b, 0, 0, 0)),
                   pl.BlockSpec((None, SUBLANES, LANES), lambda b: (b, 0, 0))],
        out_shape=[jax.ShapeDtypeStruct((n_b, NSA_KV_HEADS, SROWS, LANES), F32),
                   jax.ShapeDtypeStruct((n_b, SUBLANES, LANES), jnp.int32)],
        compiler_params=_cparams(("arbitrary",)),
        name="nsa_sample_a",
    )(qh, kcv, kcv, ov), n_slc


def _nsa_sample_b_kernel(pt_ref, sel_ref, qh_ref, cache_ref, new_ref, win_ref, oc_ref, gates_ref, gate_ref,
                         o_ref, kbuf_ref, vbuf_ref, sem_ref, *, past_len, n_slc, n_sel, n_b):
    b = pl.program_id(0)
    slot = b % 2
    n_blocks_cached = past_len // SLC_L
    per_page = PAGE_SIZE // SLC_L

    def copies(b_, slot_):
        out = []
        for h in range(NSA_KV_HEADS):
            for k in range(n_sel):
                j = jnp.minimum(sel_ref[b_, h, k], n_blocks_cached - 1)
                page = pt_ref[b_, j // per_page]
                for which, buf in ((2, kbuf_ref), (3, vbuf_ref)):
                    out.append(pltpu.make_async_copy(
                        cache_ref.at[page, pl.ds(which * KV_DIM + (h // 2) * LANES, LANES), :],
                        buf.at[slot_, h, :, pl.ds(k * PAGE_SIZE, PAGE_SIZE)],
                        sem_ref.at[slot_]))
        return out

    @pl.when(b == 0)
    def _():
        for cp in copies(b, slot):
            cp.start()

    @pl.when(b + 1 < n_b)
    def _():
        for cp in copies(b + 1, 1 - slot):
            cp.start()

    for cp in copies(b, slot):
        cp.wait()

    qpos = past_len
    nk = n_sel * PAGE_SIZE
    lane_k = lax.broadcasted_iota(jnp.int32, (1, nk), 1)
    n_buf = win_ref.shape[1]
    widx = lax.broadcasted_iota(jnp.int32, (1, n_buf), 1)
    rel = n_buf - widx
    wmask = (rel >= 0) & (rel < WINDOW) & (past_len - n_buf + widx >= 0)
    rowg = lax.broadcasted_iota(jnp.int32, (SROWS, LANES), 0)

    for h in range(NSA_KV_HEADS):
        pr = h // 2
        ls = slice(pr * LANES, (pr + 1) * LANES)
        qf = qh_ref[h] * NSA_DH ** -0.5
        qh = qf.astype(BF16)

        def new_key(row):
            return new_ref[row:row + 1, ls].astype(BF16).astype(F32)

        blk = jnp.zeros((1, nk), jnp.int32)
        n_cur = jnp.int32(0)
        for k in range(n_sel):
            sj = sel_ref[b, h, k]
            blk = blk + jnp.where(lane_k // PAGE_SIZE == k, sj, 0)
            n_cur = n_cur + (sj == qpos // SLC_L).astype(jnp.int32)
        has_new = (jnp.zeros((1, 1), jnp.int32) + n_cur) > 0
        kvalid = ((lane_k % PAGE_SIZE) // SLC_L == blk % per_page) & (blk < n_blocks_cached)
        s = _dot(qh, kbuf_ref[slot, h].astype(BF16))
        s = jnp.where(kvalid, s, NEG)
        s_new = jnp.sum(qf * new_key(2), axis=-1, keepdims=True)
        s_new = jnp.where(has_new, s_new, NEG)
        m = jnp.maximum(jnp.max(s, axis=-1, keepdims=True), s_new)
        p = jnp.where(kvalid, jnp.exp(s - m), 0.0)
        p_new = jnp.where(has_new, jnp.exp(s_new - m), 0.0)
        den = jnp.sum(p, axis=-1, keepdims=True) + p_new
        den = jnp.where(den > 0, den, 1.0)
        o_s = (_dot_nt(p.astype(BF16), vbuf_ref[slot, h].astype(BF16)) + p_new * new_key(3)) / den

        s = _dot(qh, win_ref[pr * LANES:(pr + 1) * LANES, :].astype(BF16))
        s = jnp.where(wmask, s, NEG)
        s_new = jnp.sum(qf * new_key(4), axis=-1, keepdims=True)
        m = jnp.maximum(jnp.max(s, axis=-1, keepdims=True), s_new)
        p = jnp.where(wmask, jnp.exp(s - m), 0.0)
        p_new = jnp.exp(s_new - m)
        den = jnp.sum(p, axis=-1, keepdims=True) + p_new
        wv = win_ref[KV_DIM + pr * LANES:KV_DIM + (pr + 1) * LANES, :].astype(BF16)
        o_w = (_dot_nt(p.astype(BF16), wv) + p_new * new_key(5)) / den

        g = jax.nn.sigmoid(gates_ref[:, h])
        o = g[0] * oc_ref[h] + g[1] * o_s + g[2] * o_w
        o_ref[h] = jnp.where(rowg < NSA_GROUP, o * _silu(gate_ref[h]), 0.0)


def _nsa_sample_b(page_table, sel, qh, cache_t, kv_new, win_t, o_c, gates_l, gate_l, past_len, n_slc):
    n_b = qh.shape[0]
    n_sel = sel.shape[2]
    n_buf = win_t.shape[2]
    kern = functools.partial(_nsa_sample_b_kernel, past_len=past_len, n_slc=n_slc, n_sel=n_sel, n_b=n_b)
    hb = (None, NSA_KV_HEADS, SROWS, LANES)
    return pl.pallas_call(
        kern,
        grid_spec=pltpu.PrefetchScalarGridSpec(
            num_scalar_prefetch=2,
            grid=(n_b,),
            in_specs=[pl.BlockSpec(hb, lambda b, pt, sl: (b, 0, 0, 0)),
                      pl.BlockSpec(memory_space=pl.ANY),
                      pl.BlockSpec((None, 6, KV_DIM), lambda b, pt, sl: (b, 0, 0)),
                      pl.BlockSpec((None, 2 * KV_DIM, n_buf), lambda b, pt, sl: (b, 0, 0)),
                      pl.BlockSpec(hb, lambda b, pt, sl: (b, 0, 0, 0)),
                      pl.BlockSpec((None, 3, NSA_KV_HEADS, SROWS, LANES), lambda b, pt, sl: (b, 0, 0, 0, 0)),
                      pl.BlockSpec(hb, lambda b, pt, sl: (b, 0, 0, 0))],
            out_specs=pl.BlockSpec(hb, lambda b, pt, sl: (b, 0, 0, 0)),
            scratch_shapes=[pltpu.VMEM((2, NSA_KV_HEADS, LANES, n_sel * PAGE_SIZE), F32),
                            pltpu.VMEM((2, NSA_KV_HEADS, LANES, n_sel * PAGE_SIZE), F32),
                            pltpu.SemaphoreType.DMA((2,))]),
        out_shape=jax.ShapeDtypeStruct((n_b, NSA_KV_HEADS, SROWS, LANES), F32),
        compiler_params=_cparams(("arbitrary",)),
        name="nsa_sample_b",
    )(page_table, sel, qh, cache_t, kv_new, win_t, o_c, gates_l, gate_l)


def _head_layout(x):
    n_b = x.shape[0]
    x = x.astype(F32)
    z = jnp.zeros_like(x)
    even = jnp.concatenate([x, z], axis=-1)
    odd = jnp.concatenate([z, x], axis=-1)
    par = (jnp.arange(NSA_KV_HEADS) % 2).reshape(1, NSA_KV_HEADS, 1, 1)
    y = jnp.where(par == 0, even, odd)
    return jnp.concatenate([y, jnp.zeros((n_b, NSA_KV_HEADS, SROWS - NSA_GROUP, LANES), F32)], axis=2)


def _sample_path(x_sample, cache_kv, cache_win, state, page_table, w_p, w_out_b, hg_lb, norm_g,
                 cmp_w1, cmp_w2, cmp_pe, ln_g, ln_b):
    n_b, t_s, _ = x_sample.shape
    assert t_s == 1, "single-token decode"
    n_pool = cache_kv.shape[0]
    n_pages = page_table.shape[1]
    past_len = n_pages * PAGE_SIZE
    xs = x_sample.reshape(n_b, D_MODEL)
    hg, q, kv, _, g = _inproj(xs, w_p, n_b)
    o_hg, s_new = _hgrn_sample(hg, state, hg_lb, norm_g)

    cache_t = jnp.transpose(cache_kv, (0, 2, 3, 4, 1)).reshape(n_pool, 4 * KV_DIM, PAGE_SIZE)
    bd, w2q = _cmp_bd_weights(cmp_w1, cmp_w2)
    kcv = _cmp_sample(cache_t, page_table, bd, w2q, cmp_w1, cmp_pe)

    qh = _head_layout(q.reshape(n_b, NSA_KV_HEADS, NSA_GROUP, NSA_DH))
    (o_c, sel), n_slc = _nsa_sample_a(qh, kcv, past_len)
    n_sel = min(N_SEL, n_slc)
    sel = sel[:, :NSA_KV_HEADS, :n_sel]

    gates = g[:, D_NSA:D_NSA + 3 * NSA_HEADS].reshape(n_b, 3, NSA_KV_HEADS, NSA_GROUP, 1)
    gates_l = _head_layout_b(jnp.broadcast_to(gates, (n_b, 3, NSA_KV_HEADS, NSA_GROUP, NSA_DH)))
    gate_l = _head_layout(g[:, :D_NSA].reshape(n_b, NSA_KV_HEADS, NSA_GROUP, NSA_DH))
    n_buf = cache_win.shape[1]
    win_t = jnp.transpose(cache_win, (0, 2, 3, 4, 1)).reshape(n_b, 2 * KV_DIM, n_buf)
    o = _nsa_sample_b(page_table, sel, qh, cache_t, kv.reshape(n_b, 6, KV_DIM), win_t, o_c, gates_l, gate_l,
                      past_len, n_slc)
    o = o.reshape(n_b, NSA_KV_HEADS, SROWS, 2, NSA_DH)[:, :, :NSA_GROUP].sum(axis=3)
    o_nsa = o.reshape(n_b, D_NSA).astype(BF16)

    y = _outproj(xs, o_hg.reshape(n_b, D_HG).astype(BF16), o_nsa, w_out_b, ln_g, ln_b, n_b)
    kv6 = kv.reshape(n_b, 1, 6, NSA_KV_HEADS, NSA_DH)
    win_cat = jnp.concatenate([cache_win, kv6[:, :, 4:].astype(cache_win.dtype)], axis=1)
    return (y.reshape(n_b, 1, D_MODEL), kv6[None, :, :, :4].astype(cache_kv.dtype), win_cat[None, :, 1:],
            s_new[None].astype(state.dtype))


def _head_layout_b(x):
    n_b = x.shape[0]
    y = _head_layout(x.reshape(n_b * 3, NSA_KV_HEADS, NSA_GROUP, NSA_DH))
    return y.reshape(n_b, 3, NSA_KV_HEADS, SROWS, LANES)


def kernel(x_prompt, x_sample, cache_kv, cache_win, state_hgrn, page_table, w_in, hg_lb, hg_norm_g,
           cmp_w1, cmp_w2, cmp_pe, w_out, ln_g, ln_b):
    assert w_in.shape[0] == 1, "single-layer decoder"
    b, t, _ = x_prompt.shape
    w_p = _prep_w_in(w_in[0])
    w_out_b = w_out[0].astype(BF16)
    w1p, w2p = _cmp_weights(cmp_w1[0], cmp_w2[0])

    xp = x_prompt.reshape(b * t, D_MODEL)
    tm = 1024 if t % 1024 == 0 else 256
    hg, q, kv, kvb, g, kvt, kwt = _inproj(xp, w_p, tm, t_rows=t)
    o_hg, s_p = _hgrn_prompt(hg, hg_lb, hg_norm_g[0], b, t)
    kcv = _cmp_prompt(kv, w1p, w2p, cmp_w1[0], cmp_pe[0], b, t)
    o_nsa = _nsa_prompt(q, kvb, kcv, g, b, t)
    y_p = _outproj(xp, o_hg.reshape(b * t, D_HG), o_nsa.reshape(b * t, D_NSA), w_out_b,
                   ln_g[0], ln_b[0], 512 if (b * t) % 512 == 0 else 256)
    wlen = min(WINDOW, t)
    kv_rows = jnp.transpose(kvt.reshape(b, 4, NSA_KV_HEADS, NSA_DH, t), (0, 4, 1, 2, 3))
    kv_win = jnp.transpose(kwt.reshape(b, 2, NSA_KV_HEADS, NSA_DH, t), (0, 4, 1, 2, 3))

    y_s, kv_s, win_s, s_s = _sample_path(x_sample, cache_kv[0], cache_win[0], state_hgrn[0], page_table, w_p,
                                         w_out_b, hg_lb, hg_norm_g[0], cmp_w1[0], cmp_w2[0], cmp_pe[0],
                                         ln_g[0], ln_b[0])
    return (y_p.reshape(b, t, D_MODEL), y_s, kv_rows[None], kv_win[None, :, t - wlen:],
            s_p[None].astype(x_prompt.dtype), kv_s, win_s, s_s)
```
